```python
import math
import jax, jax.numpy as jnp
from jax import lax
import numpy as np

D_MODEL = 1024
BATCH = 4
SEQ = 4096
DEPTH = 2
DEC_BATCH = 32
DEC_SEQ = 1
PAST_LEN = 16384
PAGE_SIZE = 128

N_HEADS = 8
HEAD_DIM = D_MODEL // (2 * N_HEADS)
V_DIM = 2 * HEAD_DIM
D_FF = 4 * D_MODEL
CONV_W = 3
Q_BLOCK = 128
N_ATTN = (DEPTH + 1) // 2
N_CONV = DEPTH // 2
EPS = 1e-6
SCALE = HEAD_DIM ** -0.5
NEG = -1e30

kernel_name = "diffattn_shortconv_hybrid_step"


def rmsnorm(x, g):
    xf = x.astype(jnp.float32)
    y = xf * lax.rsqrt(jnp.mean(xf * xf, axis=-1, keepdims=True) + EPS)
    return (y * g.astype(jnp.float32)).astype(x.dtype)


def lambda_init(layer):
    return 0.8 - 0.6 * math.exp(-0.3 * layer)


def diff_lambda(lq1, lk1, lq2, lk2, lam_init):
    d1 = jnp.sum(lq1.astype(jnp.float32) * lk1.astype(jnp.float32))
    d2 = jnp.sum(lq2.astype(jnp.float32) * lk2.astype(jnp.float32))
    return jnp.exp(d1) - jnp.exp(d2) + lam_init


def diff_qkv(h, w_qkv, q_gain, k_gain):
    b, s, _ = h.shape
    q, k, v = jnp.split(h @ w_qkv, 3, axis=-1)
    q = rmsnorm(q.reshape(b, s, N_HEADS, 2, HEAD_DIM), q_gain)
    k = rmsnorm(k.reshape(b, s, N_HEADS, 2, HEAD_DIM), k_gain)
    v = v.reshape(b, s, N_HEADS, V_DIM)
    return q, k, v


def diff_attend(q, q_pos, segments, lam):
    scores = []
    for k, _, k_pos in segments:
        s = jnp.einsum('bqhcd,bkhcd->bhcqk', q, k, preferred_element_type=jnp.float32) * SCALE
        scores.append(jnp.where(k_pos[None, :] <= q_pos[:, None], s, NEG))
    p = jax.nn.softmax(jnp.concatenate(scores, axis=-1), axis=-1)
    a = p[:, :, 0] - lam * p[:, :, 1]
    bounds = np.cumsum([seg[0].shape[1] for seg in segments])[:-1].tolist()
    parts = jnp.split(a, bounds, axis=-1)
    out = None
    for ap, (_, v, _) in zip(parts, segments):
        o = jnp.einsum('bhqk,bkhe->bqhe', ap, v, preferred_element_type=jnp.float32)
        out = o if out is None else out + o
    return out


def diff_out(o, subln, lam_init, w_o, dtype):
    b, s = o.shape[:2]
    o = rmsnorm(o, subln) * (1.0 - lam_init)
    return o.reshape(b, s, N_HEADS * V_DIM).astype(dtype) @ w_o


def attn_prompt(h, w_qkv, q_gain, k_gain, lam, subln, lam_init, w_o):
    b, s, _ = h.shape
    q, k, v = diff_qkv(h, w_qkv, q_gain, k_gain)
    nb = s // Q_BLOCK
    qb = jnp.moveaxis(q.reshape(b, nb, Q_BLOCK, N_HEADS, 2, HEAD_DIM), 1, 0)
    k_pos = jnp.arange(s)

    def block(args):
        qi, i = args
        q_pos = i * Q_BLOCK + jnp.arange(Q_BLOCK)
        return diff_attend(qi, q_pos, [(k, v, k_pos)], lam)

    o = lax.map(block, (qb, jnp.arange(nb)))
    o = jnp.moveaxis(o, 0, 1).reshape(b, s, N_HEADS, V_DIM)
    return diff_out(o, subln, lam_init, w_o, h.dtype), k, v


def attn_sample(h, cache_k, cache_v, page_table, w_qkv, q_gain, k_gain, lam, subln, lam_init, w_o):
    b, s, _ = h.shape
    q, k, v = diff_qkv(h, w_qkv, q_gain, k_gain)
    n_pages = page_table.shape[1]
    past = n_pages * PAGE_SIZE
    kp = cache_k[page_table].reshape(b, past, N_HEADS, 2, HEAD_DIM)
    vp = cache_v[page_table].reshape(b, past, N_HEADS, V_DIM)
    q_pos = past + jnp.arange(s)
    segs = [(kp, vp, jnp.arange(past)), (k, v, past + jnp.arange(s))]
    o = diff_attend(q, q_pos, segs, lam)
    return diff_out(o, subln, lam_init, w_o, h.dtype), k, v


def conv_gates(h, w_in):
    g_b, g_c, xt = jnp.split(h @ w_in, 3, axis=-1)
    return g_b, g_c * xt


def causal_dwconv(u_ext, w):
    s = u_ext.shape[1] - (CONV_W - 1)
    return sum(w[j] * u_ext[:, j:j + s] for j in range(CONV_W))


def conv_prompt_mixer(h, w_in, conv_w, w_out):
    g_b, u = conv_gates(h, w_in)
    u_ext = jnp.pad(u, ((0, 0), (CONV_W - 1, 0), (0, 0)))
    y = (g_b * causal_dwconv(u_ext, conv_w)) @ w_out
    return y, u_ext[:, -(CONV_W - 1):]


def conv_sample_mixer(h, state, w_in, conv_w, w_out):
    g_b, u = conv_gates(h, w_in)
    u_ext = jnp.concatenate([state.astype(u.dtype), u], axis=1)
    y = (g_b * causal_dwconv(u_ext, conv_w)) @ w_out
    return y, u_ext[:, -(CONV_W - 1):]


def sqrelu_mlp(h, w_up, w_down):
    return jnp.square(jax.nn.relu(h @ w_up)) @ w_down


def setup_inputs(seed: int = 0) -> dict:
    key = jax.random.key(seed)
    ks = jax.random.split(key, 24)
    n_pages = PAST_LEN // PAGE_SIZE
    n_used = DEC_BATCH * n_pages
    n_pool = (n_used * 5) // 4
    nrm = jax.random.normal
    f = jnp.float32
    d = D_MODEL
    return {
        "x_prompt": nrm(ks[0], (BATCH, SEQ, d), f),
        "x_sample": nrm(ks[1], (DEC_BATCH, DEC_SEQ, d), f),
        "cache_k": nrm(ks[2], (N_ATTN, n_pool, PAGE_SIZE, N_HEADS, 2, HEAD_DIM), f),
        "cache_v": nrm(ks[3], (N_ATTN, n_pool, PAGE_SIZE, N_HEADS, V_DIM), f),
        "state_conv": nrm(ks[4], (N_CONV, DEC_BATCH, CONV_W - 1, d), f),
        "page_table": jax.random.permutation(ks[5], n_pool)[:n_used].reshape(DEC_BATCH, n_pages).astype(jnp.int32),
        "norm_mix": 1.0 + 0.02 * nrm(ks[6], (DEPTH, d), f),
        "w_qkv": nrm(ks[7], (N_ATTN, d, 3 * d), f) * d ** -0.5,
        "q_gain": 1.0 + 0.02 * nrm(ks[8], (N_ATTN, HEAD_DIM), f),
        "k_gain": 1.0 + 0.02 * nrm(ks[9], (N_ATTN, HEAD_DIM), f),
        "lambda_q1": 0.1 * nrm(ks[10], (N_ATTN, HEAD_DIM), f),
        "lambda_k1": 0.1 * nrm(ks[11], (N_ATTN, HEAD_DIM), f),
        "lambda_q2": 0.1 * nrm(ks[12], (N_ATTN, HEAD_DIM), f),
        "lambda_k2": 0.1 * nrm(ks[13], (N_ATTN, HEAD_DIM), f),
        "subln": 1.0 + 0.02 * nrm(ks[14], (N_ATTN, V_DIM), f),
        "w_o": nrm(ks[15], (N_ATTN, d, d), f) * d ** -0.5,
        "w_in": nrm(ks[16], (N_CONV, d, 3 * d), f) * d ** -0.5,
        "conv_w": nrm(ks[17], (N_CONV, CONV_W, d), f) * CONV_W ** -0.5,
        "w_out": nrm(ks[18], (N_CONV, d, d), f) * d ** -0.5,
        "norm_mlp": 1.0 + 0.02 * nrm(ks[19], (DEPTH, d), f),
        "w_up": nrm(ks[20], (DEPTH, d, D_FF), f) * d ** -0.5,
        "w_down": nrm(ks[21], (DEPTH, D_FF, d), f) * D_FF ** -0.5,
    }


def reference(x_prompt, x_sample, cache_k, cache_v, state_conv, page_table,
              norm_mix, w_qkv, q_gain, k_gain, lambda_q1, lambda_k1, lambda_q2, lambda_k2,
              subln, w_o, w_in, conv_w, w_out, norm_mlp, w_up, w_down):
    xp, xs = x_prompt, x_sample
    kp_l, vp_l, ks_l, vs_l, cp_l, cs_l = [], [], [], [], [], []
    for i in range(DEPTH):
        hp = rmsnorm(xp, norm_mix[i])
        hs = rmsnorm(xs, norm_mix[i])
        if i % 2 == 0:
            a = i // 2
            lam_init = lambda_init(i)
            lam = diff_lambda(lambda_q1[a], lambda_k1[a], lambda_q2[a], lambda_k2[a], lam_init)
            yp, kp, vp = attn_prompt(hp, w_qkv[a], q_gain[a], k_gain[a], lam, subln[a], lam_init, w_o[a])
            ys, kn, vn = attn_sample(hs, cache_k[a], cache_v[a], page_table, w_qkv[a], q_gain[a],
                                     k_gain[a], lam, subln[a], lam_init, w_o[a])
            kp_l.append(kp); vp_l.append(vp); ks_l.append(kn); vs_l.append(vn)
        else:
            c = i // 2
            yp, cp = conv_prompt_mixer(hp, w_in[c], conv_w[c], w_out[c])
            ys, cs = conv_sample_mixer(hs, state_conv[c], w_in[c], conv_w[c], w_out[c])
            cp_l.append(cp); cs_l.append(cs)
        xp = xp + yp
        xs = xs + ys
        xp = xp + sqrelu_mlp(rmsnorm(xp, norm_mlp[i]), w_up[i], w_down[i])
        xs = xs + sqrelu_mlp(rmsnorm(xs, norm_mlp[i]), w_up[i], w_down[i])
    k_prompt = jnp.stack(kp_l)
    v_prompt = jnp.stack(vp_l)
    k_sample = jnp.stack(ks_l)
    v_sample = jnp.stack(vs_l)
    conv_prompt = jnp.stack(cp_l)
    conv_sample = jnp.stack(cs_l)
    return (xp, xs, k_prompt, v_prompt, k_sample, v_sample, conv_prompt, conv_sample)
```

```python
import functools
import math

import jax
import jax.numpy as jnp
from jax import lax
from jax.experimental import pallas as pl
from jax.experimental.pallas import tpu as pltpu

F32 = jnp.float32
BF16 = jnp.bfloat16

N_HEADS = 8
HEAD_DIM = 64
V_DIM = 2 * HEAD_DIM
D_MODEL = N_HEADS * V_DIM
EPS = 1e-6
SCALE = HEAD_DIM ** -0.5
NEG = -1e30
PAGE_SIZE = 128

V7X_MXU_DIM = 256
V7X_VMEM_LIMIT_BYTES = 56 * 1024 * 1024

ROW_BLOCK = 512
ATTN_BLOCK = 512
PAGES_PER_STEP = 8
FF_CHUNK = 1024


def _lambda_init(layer):
    return 0.8 - 0.6 * math.exp(-0.3 * layer)


def _rms(x, g):
    ms = jnp.mean(x * x, axis=-1, keepdims=True)
    return x * lax.rsqrt(ms + EPS) * g


def _dot(a, b):
    return jnp.dot(a, b, preferred_element_type=F32)


def _dot_nt(a, b):
    return lax.dot_general(a, b, (((1,), (1,)), ((), ())), preferred_element_type=F32)


def _const_spec(shape):
    nd = len(shape)
    return pl.BlockSpec(shape, lambda *_: (0,) * nd, pipeline_mode=pl.Buffered(1))


def _params(*sem):
    return pltpu.CompilerParams(dimension_semantics=sem, vmem_limit_bytes=V7X_VMEM_LIMIT_BYTES)


def _group_mean_square(t, gmat):
    sq = t * t
    hi = sq.astype(BF16)
    lo = (sq - hi.astype(F32)).astype(BF16)
    parts = []
    for c in range(t.shape[1] // V7X_MXU_DIM):
        sl = slice(c * V7X_MXU_DIM, (c + 1) * V7X_MXU_DIM)
        parts.append(_dot(hi[:, sl], gmat) + _dot(lo[:, sl], gmat))
    return jnp.concatenate(parts, axis=1) * (1.0 / HEAD_DIM)


def _qkv_kernel(x_ref, g_ref, w_ref, qg_ref, kg_ref, gmat_ref, q_ref, kf_ref, kb_ref, vf_ref, *vt_ref):
    d = D_MODEL
    h = _rms(x_ref[...], g_ref[...]).astype(BF16)
    gmat = gmat_ref[...]
    tq = _dot(h, w_ref[:, 0:d])
    qn = tq * lax.rsqrt(_group_mean_square(tq, gmat) + EPS) * qg_ref[...]
    q_ref[...] = qn.astype(BF16)
    tk = _dot(h, w_ref[:, d:2 * d])
    kn = tk * lax.rsqrt(_group_mean_square(tk, gmat) + EPS) * kg_ref[...]
    kf_ref[...] = kn
    kb_ref[...] = kn.astype(BF16)
    tv = _dot(h, w_ref[:, 2 * d:3 * d])
    vf_ref[...] = tv
    if vt_ref:
        vt_ref[0][...] = tv.T.astype(BF16)


def _qkv(x, g, w, qg, kg, gmat, tm, seq_blocks):
    m, d = x.shape
    row = pl.BlockSpec((tm, d), lambda i: (i, 0))
    out_shape = [jax.ShapeDtypeStruct((m, d), BF16), jax.ShapeDtypeStruct((m, d), F32),
                 jax.ShapeDtypeStruct((m, d), BF16), jax.ShapeDtypeStruct((m, d), F32)]
    out_specs = [row, row, row, row]
    if seq_blocks:
        nb = seq_blocks
        out_shape.append(jax.ShapeDtypeStruct((m // (tm * nb), nb, d, tm), BF16))
        out_specs.append(pl.BlockSpec((None, None, d, tm), lambda i: (i // nb, i % nb, 0, 0)))
    return pl.pallas_call(
        _qkv_kernel,
        grid=(m // tm,),
        in_specs=[row, _const_spec((1, d)), _const_spec((d, 3 * d)), _const_spec((1, d)),
                  _const_spec((1, d)), _const_spec((V7X_MXU_DIM, V7X_MXU_DIM))],
        out_specs=out_specs,
        out_shape=out_shape,
        compiler_params=_params("parallel"),
        name="qkv",
    )(x, g, w, qg, kg, gmat)


def _diff_lambda(lq1, lk1, lq2, lk2, lam_init):
    d1 = jnp.sum(lq1 * lk1, axis=-1, keepdims=True)
    d2 = jnp.sum(lq2 * lk2, axis=-1, keepdims=True)
    return jnp.exp(d1) - jnp.exp(d2) + lam_init


def _attn_kernel(q_ref, k_ref, vt_ref, sub_ref, lq1_ref, lk1_ref, lq2_ref, lk2_ref, o_ref,
                 m_scr, l_scr, acc_scr, *, blk, lam_init):
    qi = pl.program_id(2)
    q = q_ref[...]
    lane = lax.broadcasted_iota(jnp.int32, q.shape, 1)
    zero = jnp.zeros_like(q)
    qc = (jnp.where(lane < HEAD_DIM, q, zero), jnp.where(lane >= HEAD_DIM, q, zero))

    m_scr[...] = jnp.full(m_scr.shape, NEG, F32)
    l_scr[...] = jnp.zeros(l_scr.shape, F32)
    acc_scr[...] = jnp.zeros(acc_scr.shape, F32)

    def update(j, masked):
        kb = k_ref[pl.ds(pl.multiple_of(j * blk, blk), blk), :]
        vb = vt_ref[j]
        for c in range(2):
            s = _dot_nt(kb, qc[c])
            if masked:
                kpos = lax.broadcasted_iota(jnp.int32, s.shape, 0)
                qpos = lax.broadcasted_iota(jnp.int32, s.shape, 1)
                s = jnp.where(kpos <= qpos, s, NEG)
            m_old = m_scr[c]
            m_new = jnp.maximum(m_old, jnp.max(s, axis=0, keepdims=True))
            alpha = jnp.exp(m_old - m_new)
            p = jnp.exp(s - m_new)
            l_scr[c] = alpha * l_scr[c] + jnp.sum(p, axis=0, keepdims=True)
            acc_scr[c] = alpha * acc_scr[c] + _dot(vb, p.astype(BF16))
            m_scr[c] = m_new

    def body(j, carry):
        update(j, masked=False)
        return carry

    lax.fori_loop(0, qi, body, 0)
    update(qi, masked=True)

    lam = _diff_lambda(lq1_ref[...], lk1_ref[...], lq2_ref[...], lk2_ref[...], lam_init)
    o_t = acc_scr[0] / l_scr[0] - lam * (acc_scr[1] / l_scr[1])
    ms = jnp.mean(o_t * o_t, axis=0, keepdims=True)
    o_n = (o_t * lax.rsqrt(ms + EPS)).T
    o_ref[...] = ((o_n * sub_ref[...]) * (1.0 - lam_init)).astype(BF16)


def _attn_prompt(q, k, vt, sub, lams, batch, seq, lam_init):
    m, d = q.shape
    blk = ATTN_BLOCK
    nb = seq // blk
    small = _const_spec((1, HEAD_DIM))
    kern = functools.partial(_attn_kernel, blk=blk, lam_init=lam_init)
    return pl.pallas_call(
        kern,
        grid=(batch, N_HEADS, nb),
        in_specs=[pl.BlockSpec((blk, V_DIM), lambda b, h, i: (b * nb + i, h)),
                  pl.BlockSpec((seq, V_DIM), lambda b, h, i: (b, h)),
                  pl.BlockSpec((None, nb, V_DIM, blk), lambda b, h, i: (b, 0, h, 0)),
                  _const_spec((1, V_DIM)), small, small, small, small],
        out_specs=pl.BlockSpec((blk, V_DIM), lambda b, h, i: (b * nb + i, h)),
        out_shape=jax.ShapeDtypeStruct((m, d), BF16),
        scratch_shapes=[pltpu.VMEM((2, 1, blk), F32), pltpu.VMEM((2, 1, blk), F32),
                        pltpu.VMEM((2, V_DIM, blk), F32)],
        compiler_params=_params("parallel", "parallel", "arbitrary"),
        name="attn_prompt",
    )(q, k, vt, sub, *lams)


def _decode_kernel(pt_ref, q_ref, kn_ref, vn_ref, sub_ref, lq1_ref, lk1_ref, lq2_ref, lk2_ref, *rest,
                   pps, lam_init):
    k_refs = rest[:pps]
    v_refs = rest[pps:2 * pps]
    o_ref, kb_scr, vb_scr, qbd_scr, m_scr, l_scr, acc_scr = rest[2 * pps:]
    c = pl.program_id(1)
    nhc = 2 * N_HEADS

    @pl.when(c == 0)
    def _init():
        row = lax.broadcasted_iota(jnp.int32, (nhc, D_MODEL), 0)
        col = lax.broadcasted_iota(jnp.int32, (nhc, D_MODEL), 1)
        q = jnp.broadcast_to(q_ref[...].astype(F32), (nhc, D_MODEL))
        qbd = jnp.where(col // HEAD_DIM == row, q, 0.0)
        qbd_scr[...] = qbd.astype(BF16)
        m_scr[...] = jnp.sum(qbd * kn_ref[...], axis=1, keepdims=True)
        l_scr[...] = jnp.ones(l_scr.shape, F32)
        vn = vn_ref[...]
        for h in range(N_HEADS):
            acc_scr[h] = jnp.broadcast_to(vn[:, h * V_DIM:(h + 1) * V_DIM], (nhc, V_DIM))

    for j in range(pps):
        tok = slice(j * PAGE_SIZE, (j + 1) * PAGE_SIZE)
        kb_scr[:, tok] = k_refs[j][...].astype(BF16)
        for h in range(N_HEADS):
            vb_scr[h, tok, :] = v_refs[j][pl.ds(h, PAGE_SIZE, stride=N_HEADS), :].astype(BF16)
    s = _dot(qbd_scr[...], kb_scr[...])
    m_old = m_scr[...]
    m_new = jnp.maximum(m_old, jnp.max(s, axis=1, keepdims=True))
    alpha = jnp.exp(m_old - m_new)
    p = jnp.exp(s - m_new)
    l_scr[...] = alpha * l_scr[...] + jnp.sum(p, axis=1, keepdims=True)
    pb = p.astype(BF16)
    for h in range(N_HEADS):
        acc_scr[h] = alpha * acc_scr[h] + _dot(pb, vb_scr[h])
    m_scr[...] = m_new

    @pl.when(c == pl.num_programs(1) - 1)
    def _finish():
        lam = _diff_lambda(lq1_ref[...], lk1_ref[...], lq2_ref[...], lk2_ref[...], lam_init)
        row = lax.broadcasted_iota(jnp.int32, (nhc, V_DIM), 0)
        inv_l = 1.0 / l_scr[...]
        outs = []
        for h in range(N_HEADS):
            a = acc_scr[h] * inv_l
            a = jnp.where(row == 2 * h, a, 0.0) - lam * jnp.where(row == 2 * h + 1, a, 0.0)
            outs.append(jnp.sum(a, axis=0, keepdims=True))
        o = jnp.concatenate(outs, axis=0)
        ms = jnp.mean(o * o, axis=1, keepdims=True)
        o_ref[...] = (((o * lax.rsqrt(ms + EPS)) * sub_ref[...]) * (1.0 - lam_init)).astype(BF16)


def _attn_sample(q, k_new, v_new, cache_kt, cache_v, page_table, sub, lams, lam_init):
    b = q.shape[0]
    d = D_MODEL
    n_pages = page_table.shape[1]
    pps = PAGES_PER_STEP
    steps = n_pages // pps
    nhc = 2 * N_HEADS
    tok = pl.BlockSpec((None, 1, d), lambda i, c, pt: (i, 0, 0))
    small = pl.BlockSpec((1, HEAD_DIM), lambda i, c, pt: (0, 0))

    def page_spec(j, shape):
        return pl.BlockSpec((None,) + shape, lambda i, c, pt: (pt[i, c * pps + j], 0, 0))

    kern = functools.partial(_decode_kernel, pps=pps, lam_init=lam_init)
    grid_spec = pltpu.PrefetchScalarGridSpec(
        num_scalar_prefetch=1,
        grid=(b, steps),
        in_specs=[tok, tok, tok, pl.BlockSpec((1, V_DIM), lambda i, c, pt: (0, 0)), small, small, small, small]
        + [page_spec(j, (d, PAGE_SIZE)) for j in range(pps)]
        + [page_spec(j, (PAGE_SIZE * N_HEADS, V_DIM)) for j in range(pps)],
        out_specs=pl.BlockSpec((None, N_HEADS, V_DIM), lambda i, c, pt: (i, 0, 0)),
        scratch_shapes=[pltpu.VMEM((d, pps * PAGE_SIZE), BF16),
                        pltpu.VMEM((N_HEADS, pps * PAGE_SIZE, V_DIM), BF16),
                        pltpu.VMEM((nhc, d), BF16), pltpu.VMEM((nhc, 1), F32),
                        pltpu.VMEM((nhc, 1), F32), pltpu.VMEM((N_HEADS, nhc, V_DIM), F32)],
    )
    return pl.pallas_call(
        kern,
        grid_spec=grid_spec,
        out_shape=jax.ShapeDtypeStruct((b, N_HEADS, V_DIM), BF16),
        compiler_params=_params("parallel", "arbitrary"),
        name="attn_sample",
    )(page_table, q, k_new, v_new, sub, *lams, *([cache_kt] * pps), *([cache_v] * pps))


def _post_kernel(x_ref, a_ref, wmix_ref, g_ref, wup_ref, wdown_ref, o_ref):
    x1 = x_ref[...] + _dot(a_ref[...], wmix_ref[...])
    h = _rms(x1, g_ref[...]).astype(BF16)
    acc = x1
    ff = wup_ref.shape[1]
    for c in range(ff // FF_CHUNK):
        sl = slice(c * FF_CHUNK, (c + 1) * FF_CHUNK)
        u = jnp.maximum(_dot(h, wup_ref[:, sl]), 0.0)
        acc = acc + _dot((u * u).astype(BF16), wdown_ref[sl, :])
    o_ref[...] = acc


def _post(x, a, w_mix, g, w_up, w_down, tm):
    m, d = x.shape
    ff = w_up.shape[1]
    row = pl.BlockSpec((tm, d), lambda i: (i, 0))
    return pl.pallas_call(
        _post_kernel,
        grid=(m // tm,),
        in_specs=[row, row, _const_spec((d, d)), _const_spec((1, d)), _const_spec((d, ff)),
                  _const_spec((ff, d))],
        out_specs=row,
        out_shape=jax.ShapeDtypeStruct((m, d), F32),
        compiler_params=_params("parallel"),
        name="post",
    )(x, a, w_mix, g, w_up, w_down)


def _conv_gates(x, g, w_ref):
    d = D_MODEL
    h = _rms(x, g).astype(BF16)
    g_b = _dot(h, w_ref[:, 0:d])
    u = _dot(h, w_ref[:, d:2 * d]) * _dot(h, w_ref[:, 2 * d:3 * d])
    return g_b, u


def _conv_prompt_kernel(x_ref, g_ref, w_ref, cw_ref, a_ref, st_ref, tail_scr, *, seq_blocks):
    i = pl.program_id(0)

    @pl.when(i % seq_blocks == 0)
    def _start_of_sequence():
        tail_scr[...] = jnp.zeros(tail_scr.shape, F32)

    g_b, u = _conv_gates(x_ref[...], g_ref[...], w_ref)
    tm = u.shape[0]
    rows = lax.broadcasted_iota(jnp.int32, u.shape, 0)
    t0 = tail_scr[0:1, :]
    t1 = tail_scr[1:2, :]
    um1 = jnp.where(rows == 0, t1, pltpu.roll(u, 1, axis=0))
    um2 = jnp.where(rows == 0, t0, jnp.where(rows == 1, t1, pltpu.roll(u, 2, axis=0)))
    cw = cw_ref[...]
    conv = cw[0:1, :] * um2 + cw[1:2, :] * um1 + cw[2:3, :] * u
    a_ref[...] = (g_b * conv).astype(BF16)
    last = u[tm - 2:tm, :]
    tail_scr[0:2, :] = last
    st_ref[...] = last


def _conv_prompt(x, g, w_in, conv_w, batch, seq, tm):
    m, d = x.shape
    nb = seq // tm
    row = pl.BlockSpec((tm, d), lambda i: (i, 0))
    kern = functools.partial(_conv_prompt_kernel, seq_blocks=nb)
    return pl.pallas_call(
        kern,
        grid=(m // tm,),
        in_specs=[row, _const_spec((1, d)), _const_spec((d, 3 * d)), _const_spec(conv_w.shape)],
        out_specs=[row, pl.BlockSpec((None, 2, d), lambda i: (i // nb, 0, 0))],
        out_shape=[jax.ShapeDtypeStruct((m, d), BF16), jax.ShapeDtypeStruct((batch, 2, d), F32)],
        scratch_shapes=[pltpu.VMEM((8, d), F32)],
        compiler_params=_params("arbitrary"),
        name="conv_prompt",
    )(x, g, w_in, conv_w)


def _conv_sample_kernel(x_ref, s0_ref, s1_ref, g_ref, w_ref, cw_ref, a_ref, u_ref):
    g_b, u = _conv_gates(x_ref[...], g_ref[...], w_ref)
    cw = cw_ref[...]
    conv = cw[0:1, :] * s0_ref[...] + cw[1:2, :] * s1_ref[...] + cw[2:3, :] * u
    a_ref[...] = (g_b * conv).astype(BF16)
    u_ref[...] = u


def _conv_sample(x, s0, s1, g, w_in, conv_w):
    m, d = x.shape
    full = _const_spec((m, d))
    return pl.pallas_call(
        _conv_sample_kernel,
        grid=(1,),
        in_specs=[full, full, full, _const_spec((1, d)), _const_spec((d, 3 * d)), _const_spec(conv_w.shape)],
        out_specs=[full, full],
        out_shape=[jax.ShapeDtypeStruct((m, d), BF16), jax.ShapeDtypeStruct((m, d), F32)],
        compiler_params=_params("arbitrary"),
        name="conv_sample",
    )(x, s0, s1, g, w_in, conv_w)


def kernel(x_prompt, x_sample, cache_k, cache_v, state_conv, page_table, norm_mix, w_qkv, q_gain, k_gain,
           lambda_q1, lambda_k1, lambda_q2, lambda_k2, subln, w_o, w_in, conv_w, w_out, norm_mlp, w_up, w_down):
    batch, seq, d = x_prompt.shape
    dec_batch, dec_seq, _ = x_sample.shape
    assert d == D_MODEL and dec_seq == 1 and seq % ATTN_BLOCK == 0 and ROW_BLOCK == ATTN_BLOCK
    n_pool = cache_k.shape[1]
    m_p = batch * seq
    m_s = dec_batch * dec_seq
    tm = ROW_BLOCK

    xp = x_prompt.reshape(m_p, d)
    xs = x_sample.reshape(m_s, d)

    lam_init = _lambda_init(0)
    g0 = norm_mix[0].reshape(1, d)
    wqkv = w_qkv[0].astype(BF16)
    reps = d // HEAD_DIM
    qg = (jnp.tile(q_gain[0], reps) * SCALE).reshape(1, d)
    kg = jnp.tile(k_gain[0], reps).reshape(1, d)
    idx = jnp.arange(V7X_MXU_DIM) // HEAD_DIM
    gmat = (idx[:, None] == idx[None, :]).astype(BF16)
    lams = [a[0].reshape(1, HEAD_DIM) for a in (lambda_q1, lambda_k1, lambda_q2, lambda_k2)]
    sub = subln[0].reshape(1, V_DIM)

    q_p, kf_p, kb_p, vf_p, vt_p = _qkv(xp, g0, wqkv, qg, kg, gmat, tm, seq // tm)
    o_p = _attn_prompt(q_p, kb_p, vt_p, sub, lams, batch, seq, lam_init)

    q_s, kf_s, _, vf_s = _qkv(xs, g0, wqkv, qg, kg, gmat, m_s, 0)
    cache_kt = jnp.transpose(cache_k[0], (0, 2, 3, 4, 1)).reshape(n_pool, d, PAGE_SIZE)
    cache_vr = cache_v[0].reshape(n_pool, PAGE_SIZE * N_HEADS, V_DIM)
    o_s = _attn_sample(q_s.reshape(m_s, 1, d), kf_s.reshape(m_s, 1, d), vf_s.reshape(m_s, 1, d),
                       cache_kt, cache_vr, page_table, sub, lams, lam_init).reshape(m_s, d)

    wo = w_o[0].astype(BF16)
    gm0 = norm_mlp[0].reshape(1, d)
    wup0 = w_up[0].astype(BF16)
    wdown0 = w_down[0].astype(BF16)
    xp = _post(xp, o_p, wo, gm0, wup0, wdown0, tm)
    xs = _post(xs, o_s, wo, gm0, wup0, wdown0, m_s)

    g1 = norm_mix[1].reshape(1, d)
    win = w_in[0].astype(BF16)
    a_p, conv_p = _conv_prompt(xp, g1, win, conv_w[0], batch, seq, tm)
    a_s, u_s = _conv_sample(xs, state_conv[0, :, 0, :], state_conv[0, :, 1, :], g1, win, conv_w[0])

    wout = w_out[0].astype(BF16)
    gm1 = norm_mlp[1].reshape(1, d)
    wup1 = w_up[1].astype(BF16)
    wdown1 = w_down[1].astype(BF16)
    xp = _post(xp, a_p, wout, gm1, wup1, wdown1, tm)
    xs = _post(xs, a_s, wout, gm1, wup1, wdown1, m_s)

    conv_s = jnp.stack([state_conv[0, :, 1, :], u_s], axis=1)

    return (xp.reshape(batch, seq, d), xs.reshape(dec_batch, dec_seq, d),
            kf_p.reshape(1, batch, seq, N_HEADS, 2, HEAD_DIM), vf_p.reshape(1, batch, seq, N_HEADS, V_DIM),
            kf_s.reshape(1, dec_batch, dec_seq, N_HEADS, 2, HEAD_DIM),
            vf_s.reshape(1, dec_batch, dec_seq, N_HEADS, V_DIM),
            conv_p.reshape(1, batch, 2, d), conv_s.reshape(1, dec_batch, 2, d))
```

```python
import functools
import math

import jax
import jax.numpy as jnp
from jax import lax
from jax.experimental import pallas as pl
from jax.experimental.pallas import tpu as pltpu

F32 = jnp.float32
BF16 = jnp.bfloat16

N_HEADS = 8
HEAD_DIM = 64
V_DIM = 2 * HEAD_DIM
D_MODEL = N_HEADS * V_DIM
EPS = 1e-6
SCALE = HEAD_DIM ** -0.5
NEG = -1e30
LOG2E = math.log2(math.e)
MAX_UNSHIFTED_LOG2_SCORE = 64.0
PAGE_SIZE = 128

V7X_MXU_DIM = 256
V7X_VMEM_LIMIT_BYTES = 56 * 1024 * 1024

ROW_BLOCK = 512
ATTN_BLOCK = 512
PAGES_PER_STEP = 8
DECODE_BUFFERS = 3
FF_CHUNK = 1024


def _lambda_init(layer):
    return 0.8 - 0.6 * math.exp(-0.3 * layer)


def _rms(x, g):
    ms = jnp.mean(x * x, axis=-1, keepdims=True)
    return x * lax.rsqrt(ms + EPS) * g


def _dot(a, b):
    return jnp.dot(a, b, preferred_element_type=F32)


def _dot_nt(a, b):
    return lax.dot_general(a, b, (((1,), (1,)), ((), ())), preferred_element_type=F32)


def _const_spec(shape):
    nd = len(shape)
    return pl.BlockSpec(shape, lambda *_: (0,) * nd, pipeline_mode=pl.Buffered(1))


def _params(*sem):
    return pltpu.CompilerParams(dimension_semantics=sem, vmem_limit_bytes=V7X_VMEM_LIMIT_BYTES)


def _group_mean_square(t, gmat):
    sq = t * t
    hi = sq.astype(BF16)
    lo = (sq - hi.astype(F32)).astype(BF16)
    parts = []
    for c in range(t.shape[1] // V7X_MXU_DIM):
        sl = slice(c * V7X_MXU_DIM, (c + 1) * V7X_MXU_DIM)
        parts.append(_dot(hi[:, sl], gmat) + _dot(lo[:, sl], gmat))
    return jnp.concatenate(parts, axis=1) * (1.0 / HEAD_DIM)


def _qkv_values(x_ref, g_ref, w_ref, qg_ref, kg_ref, gmat_ref):
    d = D_MODEL
    h = _rms(x_ref[...], g_ref[...]).astype(BF16)
    gmat = gmat_ref[...]
    tq = _dot(h, w_ref[:, 0:d])
    qn = tq * lax.rsqrt(_group_mean_square(tq, gmat) + EPS) * qg_ref[...]
    tk = _dot(h, w_ref[:, d:2 * d])
    kn = tk * lax.rsqrt(_group_mean_square(tk, gmat) + EPS) * kg_ref[...]
    tv = _dot(h, w_ref[:, 2 * d:3 * d])
    return qn, kn, tv


def _qkv_prompt_kernel(x_ref, g_ref, w_ref, qg_ref, kg_ref, gmat_ref, q_ref, kb_ref, kt_ref, vf_ref, vt_ref):
    qn, kn, tv = _qkv_values(x_ref, g_ref, w_ref, qg_ref, kg_ref, gmat_ref)
    q_ref[...] = qn.astype(BF16)
    kb_ref[...] = kn.astype(BF16)
    kt_ref[...] = kn.T
    vf_ref[...] = tv
    vt_ref[...] = tv.T.astype(BF16)


def _qkv_sample_kernel(x_ref, g_ref, w_ref, qg_ref, kg_ref, gmat_ref, q_ref, kf_ref, vf_ref):
    qn, kn, tv = _qkv_values(x_ref, g_ref, w_ref, qg_ref, kg_ref, gmat_ref)
    q_ref[...] = qn.astype(BF16)
    kf_ref[...] = kn
    vf_ref[...] = tv


def _qkv_in_specs(tm, d):
    return [pl.BlockSpec((tm, d), lambda i: (i, 0)), _const_spec((1, d)), _const_spec((d, 3 * d)),
            _const_spec((1, d)), _const_spec((1, d)), _const_spec((V7X_MXU_DIM, V7X_MXU_DIM))]


def _qkv_prompt(x, g, w, qg, kg, gmat, tm, nb):
    m, d = x.shape
    batch = m // (tm * nb)
    row = pl.BlockSpec((tm, d), lambda i: (i, 0))
    return pl.pallas_call(
        _qkv_prompt_kernel,
        grid=(m // tm,),
        in_specs=_qkv_in_specs(tm, d),
        out_specs=[row, row, pl.BlockSpec((None, d, tm), lambda i: (i // nb, 0, i % nb)), row,
                   pl.BlockSpec((None, None, d, tm), lambda i: (i // nb, i % nb, 0, 0))],
        out_shape=[jax.ShapeDtypeStruct((m, d), BF16), jax.ShapeDtypeStruct((m, d), BF16),
                   jax.ShapeDtypeStruct((batch, d, tm * nb), F32), jax.ShapeDtypeStruct((m, d), F32),
                   jax.ShapeDtypeStruct((batch, nb, d, tm), BF16)],
        compiler_params=_params("parallel"),
        name="qkv_prompt",
    )(x, g, w, qg, kg, gmat)


def _qkv_sample(x, g, w, qg, kg, gmat):
    m, d = x.shape
    row = pl.BlockSpec((m, d), lambda i: (i, 0))
    return pl.pallas_call(
        _qkv_sample_kernel,
        grid=(1,),
        in_specs=_qkv_in_specs(m, d),
        out_specs=[row, row, row],
        out_shape=[jax.ShapeDtypeStruct((m, d), BF16), jax.ShapeDtypeStruct((m, d), F32),
                   jax.ShapeDtypeStruct((m, d), F32)],
        compiler_params=_params("parallel"),
        name="qkv_sample",
    )(x, g, w, qg, kg, gmat)


def _diff_lambda(lq1, lk1, lq2, lk2, lam_init):
    d1 = jnp.sum(lq1 * lk1, axis=-1, keepdims=True)
    d2 = jnp.sum(lq2 * lk2, axis=-1, keepdims=True)
    return jnp.exp(d1) - jnp.exp(d2) + lam_init


def _split_components(q):
    lane = lax.broadcasted_iota(jnp.int32, q.shape, 1)
    zero = jnp.zeros_like(q)
    return jnp.where(lane < HEAD_DIM, q, zero), jnp.where(lane >= HEAD_DIM, q, zero)


def _causal_keep(shape):
    kpos = lax.broadcasted_iota(jnp.int32, shape, 0)
    qpos = lax.broadcasted_iota(jnp.int32, shape, 1)
    return kpos <= qpos


def _attn_finish(acc0, l0, acc1, l1, sub_ref, lam_refs, o_ref, lam_init):
    lam = _diff_lambda(*(r[...] for r in lam_refs), lam_init)
    o_t = acc0 / l0 - lam * (acc1 / l1)
    ms = jnp.mean(o_t * o_t, axis=0, keepdims=True)
    o_n = (o_t * lax.rsqrt(ms + EPS)).T
    o_ref[...] = ((o_n * sub_ref[...]) * (1.0 - lam_init)).astype(BF16)


def _attn_bounded_kernel(q_ref, k_ref, vt_ref, sub_ref, lq1_ref, lk1_ref, lq2_ref, lk2_ref, o_ref,
                         s_even, s_odd, l_scr, acc_scr, *, blk, lam_init):
    half = blk // 2
    qi = pl.program_id(2)
    qc = _split_components(q_ref[...])
    l_scr[...] = jnp.zeros(l_scr.shape, F32)
    acc_scr[...] = jnp.zeros(acc_scr.shape, F32)

    def scores(j, hf, dst):
        kb = k_ref[pl.ds(pl.multiple_of(j * blk + hf * half, half), half), :]
        for c in range(2):
            dst[c] = _dot_nt(kb, qc[c])

    def consume(j, hf, src, masked):
        vb = vt_ref[j, :, hf * half:(hf + 1) * half]
        for c in range(2):
            p = jnp.exp2(src[c])
            if masked:
                kpos = lax.broadcasted_iota(jnp.int32, p.shape, 0) + hf * half
                qpos = lax.broadcasted_iota(jnp.int32, p.shape, 1)
                p = jnp.where(kpos <= qpos, p, 0.0)
            l_scr[c] += jnp.sum(p.reshape(half // 8, 8, blk), axis=0)
            acc_scr[c] += _dot(vb, p.astype(BF16))

    scores(0, 0, s_even)

    def body(j, carry):
        scores(j, 1, s_odd)
        consume(j, 0, s_even, masked=False)
        scores(j + 1, 0, s_even)
        consume(j, 1, s_odd, masked=False)
        return carry

    lax.fori_loop(0, qi, body, 0)
    scores(qi, 1, s_odd)
    consume(qi, 0, s_even, masked=True)
    consume(qi, 1, s_odd, masked=True)
    l0 = jnp.sum(l_scr[0], axis=0, keepdims=True)
    l1 = jnp.sum(l_scr[1], axis=0, keepdims=True)
    _attn_finish(acc_scr[0], l0, acc_scr[1], l1, sub_ref, (lq1_ref, lk1_ref, lq2_ref, lk2_ref), o_ref, lam_init)


def _attn_online_kernel(q_ref, k_ref, vt_ref, sub_ref, lq1_ref, lk1_ref, lq2_ref, lk2_ref, o_ref,
                        m_scr, l_scr, acc_scr, *, blk, lam_init):
    qi = pl.program_id(2)
    qc = _split_components(q_ref[...])
    m_scr[...] = jnp.full(m_scr.shape, NEG, F32)
    l_scr[...] = jnp.zeros(l_scr.shape, F32)
    acc_scr[...] = jnp.zeros(acc_scr.shape, F32)

    def update(j, masked):
        kb = k_ref[pl.ds(pl.multiple_of(j * blk, blk), blk), :]
        vb = vt_ref[j]
        for c in range(2):
            s = _dot_nt(kb, qc[c])
            if masked:
                s = jnp.where(_causal_keep(s.shape), s, NEG)
            m_old = m_scr[c]
            m_new = jnp.maximum(m_old, jnp.max(s, axis=0, keepdims=True))
            alpha = jnp.exp2(m_old - m_new)
            p = jnp.exp2(s - m_new)
            l_scr[c] = alpha * l_scr[c] + jnp.sum(p, axis=0, keepdims=True)
            acc_scr[c] = alpha * acc_scr[c] + _dot(vb, p.astype(BF16))
            m_scr[c] = m_new

    def body(j, carry):
        update(j, masked=False)
        return carry

    lax.fori_loop(0, qi, body, 0)
    update(qi, masked=True)
    _attn_finish(acc_scr[0], l_scr[0], acc_scr[1], l_scr[1], sub_ref, (lq1_ref, lk1_ref, lq2_ref, lk2_ref),
                 o_ref, lam_init)


def _attn_prompt(q, k, vt, sub, lams, batch, seq, lam_init, bounded):
    m, d = q.shape
    blk = ATTN_BLOCK
    nb = seq // blk
    small = _const_spec((1, HEAD_DIM))
    if bounded:
        stage = pltpu.VMEM((2, blk // 2, blk), F32)
        stats = [stage, stage, pltpu.VMEM((2, 8, blk), F32)]
    else:
        stats = [pltpu.VMEM((2, 1, blk), F32)] * 2
    kern = _attn_bounded_kernel if bounded else _attn_online_kernel
    return pl.pallas_call(
        functools.partial(kern, blk=blk, lam_init=lam_init),
        grid=(batch, N_HEADS, nb),
        in_specs=[pl.BlockSpec((blk, V_DIM), lambda b, h, i: (b * nb + i, h)),
                  pl.BlockSpec((seq, V_DIM), lambda b, h, i: (b, h)),
                  pl.BlockSpec((None, nb, V_DIM, blk), lambda b, h, i: (b, 0, h, 0)),
                  _const_spec((1, V_DIM)), small, small, small, small],
        out_specs=pl.BlockSpec((blk, V_DIM), lambda b, h, i: (b * nb + i, h)),
        out_shape=jax.ShapeDtypeStruct((m, d), BF16),
        scratch_shapes=stats + [pltpu.VMEM((2, V_DIM, blk), F32)],
        compiler_params=_params("parallel", "parallel", "arbitrary"),
        name="attn_prompt_bounded" if bounded else "attn_prompt_online",
    )(q, k, vt, sub, *lams)


def _decode_kernel(pt_ref, q_ref, kn_ref, vn_ref, sub_ref, lq1_ref, lk1_ref, lq2_ref, lk2_ref,
                   kt_hbm, v_hbm, o_ref, kbuf, vbuf, sem, kb_scr, vb_scr, qbd_scr, m_scr, l_scr, acc_scr,
                   *, pps, steps, nbuf, lam_init):
    nhc = 2 * N_HEADS
    total = q_ref.shape[0] * steps

    def chunk_copies(t, slot):
        b = t // steps
        c = t % steps
        copies = []
        for j in range(pps):
            page = pt_ref[b, c * pps + j]
            copies.append(pltpu.make_async_copy(kt_hbm.at[page], kbuf.at[slot, j], sem.at[slot]))
            copies.append(pltpu.make_async_copy(v_hbm.at[page], vbuf.at[slot, j], sem.at[slot]))
        return copies

    def init(b):
        row = lax.broadcasted_iota(jnp.int32, (nhc, D_MODEL), 0)
        col = lax.broadcasted_iota(jnp.int32, (nhc, D_MODEL), 1)
        q = jnp.broadcast_to(q_ref[b].astype(F32), (nhc, D_MODEL))
        qbd = jnp.where(col // HEAD_DIM == row, q, 0.0)
        qbd_scr[...] = qbd.astype(BF16)
        m_scr[...] = jnp.sum(qbd * kn_ref[b], axis=1, keepdims=True)
        l_scr[...] = jnp.ones(l_scr.shape, F32)
        vn = jnp.broadcast_to(vn_ref[b], (nhc, D_MODEL))
        for hp in range(N_HEADS // 2):
            acc_scr[hp] = vn[:, hp * 2 * V_DIM:(hp + 1) * 2 * V_DIM]

    def accumulate(slot):
        for j in range(pps):
            tok = slice(j * PAGE_SIZE, (j + 1) * PAGE_SIZE)
            kb_scr[:, tok] = kbuf[slot, j].astype(BF16)
            for h in range(N_HEADS):
                lanes = slice((h % 2) * V_DIM, (h % 2 + 1) * V_DIM)
                vb_scr[h // 2, tok, lanes] = vbuf[slot, j, pl.ds(h, PAGE_SIZE, stride=N_HEADS), :].astype(BF16)
        s = _dot(qbd_scr[...], kb_scr[...])
        m_old = m_scr[...]
        m_new = jnp.maximum(m_old, jnp.max(s, axis=1, keepdims=True))
        alpha = jnp.exp2(m_old - m_new)
        p = jnp.exp2(s - m_new)
        l_scr[...] = alpha * l_scr[...] + jnp.sum(p, axis=1, keepdims=True)
        pb = p.astype(BF16)
        for hp in range(N_HEADS // 2):
            acc_scr[hp] = alpha * acc_scr[hp] + _dot(pb, vb_scr[hp])
        m_scr[...] = m_new

    def finish(b):
        lam = _diff_lambda(lq1_ref[...], lk1_ref[...], lq2_ref[...], lk2_ref[...], lam_init)
        row = lax.broadcasted_iota(jnp.int32, (nhc, V_DIM), 0)
        inv_l = 1.0 / l_scr[...]
        outs = []
        for h in range(N_HEADS):
            a = acc_scr[h // 2][:, (h % 2) * V_DIM:(h % 2 + 1) * V_DIM] * inv_l
            a = jnp.where(row == 2 * h, a, 0.0) - lam * jnp.where(row == 2 * h + 1, a, 0.0)
            outs.append(jnp.sum(a, axis=0, keepdims=True))
        o = jnp.concatenate(outs, axis=0)
        ms = jnp.mean(o * o, axis=1, keepdims=True)
        o_ref[b] = ((o * lax.rsqrt(ms + EPS)) * sub_ref[...]) * (1.0 - lam_init)

    t = pl.program_id(0)
    slot = lax.rem(t, nbuf)
    ahead = t + (nbuf - 1)

    @pl.when(t == 0)
    def _prologue():
        for t0 in range(nbuf - 1):
            for cp in chunk_copies(t0, t0):
                cp.start()

    @pl.when(ahead < total)
    def _prefetch():
        for cp in chunk_copies(ahead, lax.rem(ahead, nbuf)):
            cp.start()

    for cp in chunk_copies(t, slot):
        cp.wait()
    b = t // steps
    c = t % steps
    pl.when(c == 0)(lambda: init(b))
    accumulate(slot)
    pl.when(c == steps - 1)(lambda: finish(b))


def _attn_sample(q, k_new, v_new, cache_kt, cache_v, page_table, sub, lams, lam_init):
    b = q.shape[0]
    d = D_MODEL
    n_pages = page_table.shape[1]
    pps = PAGES_PER_STEP
    nbuf = DECODE_BUFFERS
    assert n_pages % pps == 0
    steps = n_pages // pps
    nhc = 2 * N_HEADS

    def whole(shape):
        nd = len(shape)
        return pl.BlockSpec(shape, lambda i, pt: (0,) * nd)

    kern = functools.partial(_decode_kernel, pps=pps, steps=steps, nbuf=nbuf, lam_init=lam_init)
    grid_spec = pltpu.PrefetchScalarGridSpec(
        num_scalar_prefetch=1,
        grid=(b * steps,),
        in_specs=[whole(q.shape), whole(k_new.shape), whole(v_new.shape), whole((1, V_DIM))]
        + [whole((1, HEAD_DIM))] * 4
        + [pl.BlockSpec(memory_space=pl.ANY), pl.BlockSpec(memory_space=pl.ANY)],
        out_specs=whole((b, N_HEADS, V_DIM)),
        scratch_shapes=[pltpu.VMEM((nbuf, pps, d, PAGE_SIZE), F32),
                        pltpu.VMEM((nbuf, pps, PAGE_SIZE * N_HEADS, V_DIM), F32),
                        pltpu.SemaphoreType.DMA((nbuf,)),
                        pltpu.VMEM((d, pps * PAGE_SIZE), BF16),
                        pltpu.VMEM((N_HEADS // 2, pps * PAGE_SIZE, 2 * V_DIM), BF16),
                        pltpu.VMEM((nhc, d), BF16), pltpu.VMEM((nhc, 1), F32),
                        pltpu.VMEM((nhc, 1), F32), pltpu.VMEM((N_HEADS // 2, nhc, 2 * V_DIM), F32)],
    )
    return pl.pallas_call(
        kern,
        grid_spec=grid_spec,
        out_shape=jax.ShapeDtypeStruct((b, N_HEADS, V_DIM), F32),
        compiler_params=_params("arbitrary"),
        name="attn_sample",
    )(page_table, q, k_new, v_new, sub, *lams, cache_kt, cache_v)


def _post_kernel(x_ref, a_ref, wmix_ref, g_ref, wup_ref, wdown_ref, o_ref):
    x1 = x_ref[...] + _dot(a_ref[...], wmix_ref[...])
    h = _rms(x1, g_ref[...]).astype(BF16)
    acc = x1
    ff = wup_ref.shape[1]
    for c in range(ff // FF_CHUNK):
        sl = slice(c * FF_CHUNK, (c + 1) * FF_CHUNK)
        u = jnp.maximum(_dot(h, wup_ref[:, sl]), 0.0)
        acc = acc + _dot((u * u).astype(BF16), wdown_ref[sl, :])
    o_ref[...] = acc


def _post(x, a, w_mix, g, w_up, w_down, tm):
    m, d = x.shape
    ff = w_up.shape[1]
    row = pl.BlockSpec((tm, d), lambda i: (i, 0))
    return pl.pallas_call(
        _post_kernel,
        grid=(m // tm,),
        in_specs=[row, row, _const_spec((d, d)), _const_spec((1, d)), _const_spec((d, ff)),
                  _const_spec((ff, d))],
        out_specs=row,
        out_shape=jax.ShapeDtypeStruct((m, d), F32),
        compiler_params=_params("parallel"),
        name="post",
    )(x, a, w_mix, g, w_up, w_down)


def _conv_gates(x, g, w_ref):
    d = D_MODEL
    h = _rms(x, g).astype(BF16)
    g_b = _dot(h, w_ref[:, 0:d])
    u = _dot(h, w_ref[:, d:2 * d]) * _dot(h, w_ref[:, 2 * d:3 * d])
    return g_b, u


def _conv_prompt_kernel(x_ref, g_ref, w_ref, cw_ref, a_ref, st_ref, tail_scr, *, seq_blocks):
    i = pl.program_id(0)

    @pl.when(i % seq_blocks == 0)
    def _start_of_sequence():
        tail_scr[...] = jnp.zeros(tail_scr.shape, F32)

    g_b, u = _conv_gates(x_ref[...], g_ref[...], w_ref)
    tm = u.shape[0]
    rows = lax.broadcasted_iota(jnp.int32, u.shape, 0)
    t0 = tail_scr[0:1, :]
    t1 = tail_scr[1:2, :]
    um1 = jnp.where(rows == 0, t1, pltpu.roll(u, 1, axis=0))
    um2 = jnp.where(rows == 0, t0, jnp.where(rows == 1, t1, pltpu.roll(u, 2, axis=0)))
    cw = cw_ref[...]
    conv = cw[0:1, :] * um2 + cw[1:2, :] * um1 + cw[2:3, :] * u
    a_ref[...] = (g_b * conv).astype(BF16)
    last = u[tm - 2:tm, :]
    tail_scr[0:2, :] = last
    st_ref[...] = last


def _conv_prompt(x, g, w_in, conv_w, batch, seq, tm):
    m, d = x.shape
    nb = seq // tm
    row = pl.BlockSpec((tm, d), lambda i: (i, 0))
    kern = functools.partial(_conv_prompt_kernel, seq_blocks=nb)
    return pl.pallas_call(
        kern,
        grid=(m // tm,),
        in_specs=[row, _const_spec((1, d)), _const_spec((d, 3 * d)), _const_spec(conv_w.shape)],
        out_specs=[row, pl.BlockSpec((None, 2, d), lambda i: (i // nb, 0, 0))],
        out_shape=[jax.ShapeDtypeStruct((m, d), BF16), jax.ShapeDtypeStruct((batch, 2, d), F32)],
        scratch_shapes=[pltpu.VMEM((8, d), F32)],
        compiler_params=_params("arbitrary"),
        name="conv_prompt",
    )(x, g, w_in, conv_w)


def _conv_sample_kernel(x_ref, s0_ref, s1_ref, g_ref, w_ref, cw_ref, a_ref, u_ref):
    g_b, u = _conv_gates(x_ref[...], g_ref[...], w_ref)
    cw = cw_ref[...]
    conv = cw[0:1, :] * s0_ref[...] + cw[1:2, :] * s1_ref[...] + cw[2:3, :] * u
    a_ref[...] = (g_b * conv).astype(BF16)
    u_ref[...] = u


def _conv_sample(x, s0, s1, g, w_in, conv_w):
    m, d = x.shape
    full = _const_spec((m, d))
    return pl.pallas_call(
        _conv_sample_kernel,
        grid=(1,),
        in_specs=[full, full, full, _const_spec((1, d)), _const_spec((d, 3 * d)), _const_spec(conv_w.shape)],
        out_specs=[full, full],
        out_shape=[jax.ShapeDtypeStruct((m, d), BF16), jax.ShapeDtypeStruct((m, d), F32)],
        compiler_params=_params("arbitrary"),
        name="conv_sample",
    )(x, s0, s1, g, w_in, conv_w)


def kernel(x_prompt, x_sample, cache_k, cache_v, state_conv, page_table, norm_mix, w_qkv, q_gain, k_gain,
           lambda_q1, lambda_k1, lambda_q2, lambda_k2, subln, w_o, w_in, conv_w, w_out, norm_mlp, w_up, w_down):
    batch, seq, d = x_prompt.shape
    dec_batch, dec_seq, _ = x_sample.shape
    assert d == D_MODEL and dec_seq == 1 and seq % ATTN_BLOCK == 0 and ROW_BLOCK == ATTN_BLOCK
    n_pool = cache_k.shape[1]
    m_p = batch * seq
    m_s = dec_batch * dec_seq
    tm = ROW_BLOCK

    xp = x_prompt.reshape(m_p, d)
    xs = x_sample.reshape(m_s, d)

    lam_init = _lambda_init(0)
    g0 = norm_mix[0].reshape(1, d)
    wqkv = w_qkv[0].astype(BF16)
    reps = d // HEAD_DIM
    qg = (jnp.tile(q_gain[0], reps) * (SCALE * LOG2E)).reshape(1, d)
    kg = jnp.tile(k_gain[0], reps).reshape(1, d)
    idx = jnp.arange(V7X_MXU_DIM) // HEAD_DIM
    gmat = (idx[:, None] == idx[None, :]).astype(BF16)
    lams = [a[0].reshape(1, HEAD_DIM) for a in (lambda_q1, lambda_k1, lambda_q2, lambda_k2)]
    sub = subln[0].reshape(1, V_DIM)

    q_p, kb_p, kt_p, vf_p, vt_p = _qkv_prompt(xp, g0, wqkv, qg, kg, gmat, tm, seq // tm)
    score_bound = HEAD_DIM * SCALE * LOG2E * jnp.max(jnp.abs(q_gain[0])) * jnp.max(jnp.abs(k_gain[0]))
    o_p = lax.cond(
        score_bound <= MAX_UNSHIFTED_LOG2_SCORE,
        lambda q, k, vt, sb, *lm: _attn_prompt(q, k, vt, sb, lm, batch, seq, lam_init, True),
        lambda q, k, vt, sb, *lm: _attn_prompt(q, k, vt, sb, lm, batch, seq, lam_init, False),
        q_p, kb_p, vt_p, sub, *lams)

    q_s, kf_s, vf_s = _qkv_sample(xs, g0, wqkv, qg, kg, gmat)
    cache_kt = jnp.transpose(cache_k[0], (0, 2, 3, 4, 1)).reshape(n_pool, d, PAGE_SIZE)
    cache_vr = cache_v[0].reshape(n_pool, PAGE_SIZE * N_HEADS, V_DIM)
    o_s = _attn_sample(q_s.reshape(m_s, 1, d), kf_s.reshape(m_s, 1, d), vf_s.reshape(m_s, 1, d),
                       cache_kt, cache_vr, page_table, sub, lams, lam_init).reshape(m_s, d).astype(BF16)

    wo = w_o[0].astype(BF16)
    gm0 = norm_mlp[0].reshape(1, d)
    wup0 = w_up[0].astype(BF16)
    wdown0 = w_down[0].astype(BF16)
    xp = _post(xp, o_p, wo, gm0, wup0, wdown0, tm)
    xs = _post(xs, o_s, wo, gm0, wup0, wdown0, m_s)

    g1 = norm_mix[1].reshape(1, d)
    win = w_in[0].astype(BF16)
    a_p, conv_p = _conv_prompt(xp, g1, win, conv_w[0], batch, seq, tm)
    a_s, u_s = _conv_sample(xs, state_conv[0, :, 0, :], state_conv[0, :, 1, :], g1, win, conv_w[0])

    wout = w_out[0].astype(BF16)
    gm1 = norm_mlp[1].reshape(1, d)
    wup1 = w_up[1].astype(BF16)
    wdown1 = w_down[1].astype(BF16)
    xp = _post(xp, a_p, wout, gm1, wup1, wdown1, tm)
    xs = _post(xs, a_s, wout, gm1, wup1, wdown1, m_s)

    conv_s = jnp.stack([state_conv[0, :, 1, :], u_s], axis=1)

    return (xp.reshape(batch, seq, d), xs.reshape(dec_batch, dec_seq, d),
            kt_p.reshape(1, batch, N_HEADS, 2, HEAD_DIM, seq).transpose(0, 1, 5, 2, 3, 4),
            vf_p.reshape(1, batch, seq, N_HEADS, V_DIM),
            kf_s.reshape(1, dec_batch, dec_seq, N_HEADS, 2, HEAD_DIM),
            vf_s.reshape(1, dec_batch, dec_seq, N_HEADS, V_DIM),
            conv_p.reshape(1, batch, 2, d), conv_s.reshape(1, dec_batch, 2, d))
```

```python
import functools
import math

import jax
import jax.numpy as jnp
from jax import lax
from jax.experimental import pallas as pl
from jax.experimental.pallas import tpu as pltpu

F32 = jnp.float32
BF16 = jnp.bfloat16

N_HEADS = 8
HEAD_DIM = 64
V_DIM = 2 * HEAD_DIM
D_MODEL = N_HEADS * V_DIM
EPS = 1e-6
SCALE = HEAD_DIM ** -0.5
NEG = -1e30
LOG2E = math.log2(math.e)
MAX_UNSHIFTED_LOG2_SCORE = 64.0
PAGE_SIZE = 128

V7X_MXU_DIM = 256
V7X_VMEM_LIMIT_BYTES = 56 * 1024 * 1024

ROW_BLOCK = 512
ATTN_BLOCK = 512
PAGES_PER_STEP = 8
DECODE_BUFFERS = 3
FUSED_DECODE_BUFFERS = 4
ATTN_FIXED_WORK = 1.7
FF_CHUNK = 1024


def _lambda_init(layer):
    return 0.8 - 0.6 * math.exp(-0.3 * layer)


def _rms(x, g):
    ms = jnp.mean(x * x, axis=-1, keepdims=True)
    return x * lax.rsqrt(ms + EPS) * g


def _dot(a, b):
    return jnp.dot(a, b, preferred_element_type=F32)


def _dot_nt(a, b):
    return lax.dot_general(a, b, (((1,), (1,)), ((), ())), preferred_element_type=F32)


def _const_spec(shape):
    nd = len(shape)
    return pl.BlockSpec(shape, lambda *_: (0,) * nd, pipeline_mode=pl.Buffered(1))


def _params(*sem):
    return pltpu.CompilerParams(dimension_semantics=sem, vmem_limit_bytes=V7X_VMEM_LIMIT_BYTES)


def _group_mean_square(t, gmat):
    sq = t * t
    hi = sq.astype(BF16)
    lo = (sq - hi.astype(F32)).astype(BF16)
    parts = []
    for c in range(t.shape[1] // V7X_MXU_DIM):
        sl = slice(c * V7X_MXU_DIM, (c + 1) * V7X_MXU_DIM)
        parts.append(_dot(hi[:, sl], gmat) + _dot(lo[:, sl], gmat))
    return jnp.concatenate(parts, axis=1) * (1.0 / HEAD_DIM)


def _qkv_values(x_ref, g_ref, w_ref, qg_ref, kg_ref, gmat_ref):
    d = D_MODEL
    h = _rms(x_ref[...], g_ref[...]).astype(BF16)
    gmat = gmat_ref[...]
    tq = _dot(h, w_ref[:, 0:d])
    qn = tq * lax.rsqrt(_group_mean_square(tq, gmat) + EPS) * qg_ref[...]
    tk = _dot(h, w_ref[:, d:2 * d])
    kn = tk * lax.rsqrt(_group_mean_square(tk, gmat) + EPS) * kg_ref[...]
    tv = _dot(h, w_ref[:, 2 * d:3 * d])
    return qn, kn, tv


def _qkv_prompt_kernel(x_ref, g_ref, w_ref, qg_ref, kg_ref, gmat_ref, q_ref, kb_ref, kt_ref, vf_ref, vt_ref):
    qn, kn, tv = _qkv_values(x_ref, g_ref, w_ref, qg_ref, kg_ref, gmat_ref)
    q_ref[...] = qn.astype(BF16)
    kb_ref[...] = kn.astype(BF16)
    kt_ref[...] = kn.T
    vf_ref[...] = tv
    vt_ref[...] = tv.T.astype(BF16)


def _qkv_sample_kernel(x_ref, g_ref, w_ref, qg_ref, kg_ref, gmat_ref, q_ref, kf_ref, vf_ref):
    qn, kn, tv = _qkv_values(x_ref, g_ref, w_ref, qg_ref, kg_ref, gmat_ref)
    q_ref[...] = qn.astype(BF16)
    kf_ref[...] = kn
    vf_ref[...] = tv


def _qkv_in_specs(tm, d):
    return [pl.BlockSpec((tm, d), lambda i: (i, 0)), _const_spec((1, d)), _const_spec((d, 3 * d)),
            _const_spec((1, d)), _const_spec((1, d)), _const_spec((V7X_MXU_DIM, V7X_MXU_DIM))]


def _qkv_prompt(x, g, w, qg, kg, gmat, tm, nb):
    m, d = x.shape
    batch = m // (tm * nb)
    row = pl.BlockSpec((tm, d), lambda i: (i, 0))
    return pl.pallas_call(
        _qkv_prompt_kernel,
        grid=(m // tm,),
        in_specs=_qkv_in_specs(tm, d),
        out_specs=[row, row, pl.BlockSpec((None, d, tm), lambda i: (i // nb, 0, i % nb)), row,
                   pl.BlockSpec((None, None, d, tm), lambda i: (i // nb, i % nb, 0, 0))],
        out_shape=[jax.ShapeDtypeStruct((m, d), BF16), jax.ShapeDtypeStruct((m, d), BF16),
                   jax.ShapeDtypeStruct((batch, d, tm * nb), F32), jax.ShapeDtypeStruct((m, d), F32),
                   jax.ShapeDtypeStruct((batch, nb, d, tm), BF16)],
        compiler_params=_params("parallel"),
        name="qkv_prompt",
    )(x, g, w, qg, kg, gmat)


def _qkv_sample(x, g, w, qg, kg, gmat):
    m, d = x.shape
    row = pl.BlockSpec((m, d), lambda i: (i, 0))
    return pl.pallas_call(
        _qkv_sample_kernel,
        grid=(1,),
        in_specs=_qkv_in_specs(m, d),
        out_specs=[row, row, row],
        out_shape=[jax.ShapeDtypeStruct((m, d), BF16), jax.ShapeDtypeStruct((m, d), F32),
                   jax.ShapeDtypeStruct((m, d), F32)],
        compiler_params=_params("parallel"),
        name="qkv_sample",
    )(x, g, w, qg, kg, gmat)


def _diff_lambda(lq1, lk1, lq2, lk2, lam_init):
    d1 = jnp.sum(lq1 * lk1, axis=-1, keepdims=True)
    d2 = jnp.sum(lq2 * lk2, axis=-1, keepdims=True)
    return jnp.exp(d1) - jnp.exp(d2) + lam_init


def _split_components(q):
    lane = lax.broadcasted_iota(jnp.int32, q.shape, 1)
    zero = jnp.zeros_like(q)
    return jnp.where(lane < HEAD_DIM, q, zero), jnp.where(lane >= HEAD_DIM, q, zero)


def _causal_keep(shape):
    kpos = lax.broadcasted_iota(jnp.int32, shape, 0)
    qpos = lax.broadcasted_iota(jnp.int32, shape, 1)
    return kpos <= qpos


def _attn_finish(acc0, l0, acc1, l1, sub_ref, lam_refs, o_ref, lam_init):
    lam = _diff_lambda(*(r[...] for r in lam_refs), lam_init)
    o_t = acc0 / l0 - lam * (acc1 / l1)
    ms = jnp.mean(o_t * o_t, axis=0, keepdims=True)
    o_n = (o_t * lax.rsqrt(ms + EPS)).T
    o_ref[...] = ((o_n * sub_ref[...]) * (1.0 - lam_init)).astype(BF16)


def _chunks_per_query_block(nb, chunks):
    weights = [ATTN_FIXED_WORK + i for i in range(nb)]
    ideal = [chunks * w / sum(weights) for w in weights]
    counts = [int(x) for x in ideal]
    by_remainder = sorted(range(nb), key=lambda i: ideal[i] - counts[i], reverse=True)
    for i in by_remainder[:chunks - sum(counts)]:
        counts[i] += 1
    return counts


def _attn_bounded_kernel(pt_ref, q_ref, k_ref, vt_ref, sub_ref, lq1_ref, lk1_ref, lq2_ref, lk2_ref,
                         qs_ref, kn_ref, vn_ref, kt_hbm, v_hbm, o_ref, os_ref,
                         s_even, s_odd, l_scr, acc_scr, *decode_scratch,
                         blk, lam_init, chunks_per_group, decode_cfg):
    half = blk // 2
    qi = pl.program_id(2)
    group = pl.program_id(0) * pl.num_programs(1) + pl.program_id(1)

    prologue, process = _decode_program(
        pt_ref, qs_ref, kn_ref, vn_ref, sub_ref, lq1_ref, lk1_ref, lq2_ref, lk2_ref, kt_hbm, v_hbm, os_ref,
        *decode_scratch, lam_init=lam_init, **decode_cfg)
    counts = _chunks_per_query_block(k_ref.shape[0] // blk, chunks_per_group)
    first = group * chunks_per_group + sum(jnp.where(qi > i, n, 0) for i, n in enumerate(counts))
    count = sum(jnp.where(qi == i, n, 0) for i, n in enumerate(counts))
    pl.when(jnp.logical_and(group == 0, qi == 0))(prologue)

    def decode_body(i, carry):
        process(first + i)
        return carry

    lax.fori_loop(0, count, decode_body, 0)

    qc = _split_components(q_ref[...])
    l_scr[...] = jnp.zeros(l_scr.shape, F32)
    acc_scr[...] = jnp.zeros(acc_scr.shape, F32)

    def scores(j, hf, dst):
        kb = k_ref[pl.ds(pl.multiple_of(j * blk + hf * half, half), half), :]
        for c in range(2):
            dst[c] = _dot_nt(kb, qc[c])

    def consume(j, hf, src, masked):
        vb = vt_ref[j, :, hf * half:(hf + 1) * half]
        for c in range(2):
            p = jnp.exp2(src[c])
            if masked:
                kpos = lax.broadcasted_iota(jnp.int32, p.shape, 0) + hf * half
                qpos = lax.broadcasted_iota(jnp.int32, p.shape, 1)
                p = jnp.where(kpos <= qpos, p, 0.0)
            l_scr[c] += jnp.sum(p.reshape(half // 8, 8, blk), axis=0)
            acc_scr[c] += _dot(vb, p.astype(BF16))

    scores(0, 0, s_even)

    def body(j, carry):
        scores(j, 1, s_odd)
        consume(j, 0, s_even, masked=False)
        scores(j + 1, 0, s_even)
        consume(j, 1, s_odd, masked=False)
        return carry

    lax.fori_loop(0, qi, body, 0)
    scores(qi, 1, s_odd)
    consume(qi, 0, s_even, masked=True)
    consume(qi, 1, s_odd, masked=True)
    l0 = jnp.sum(l_scr[0], axis=0, keepdims=True)
    l1 = jnp.sum(l_scr[1], axis=0, keepdims=True)
    _attn_finish(acc_scr[0], l0, acc_scr[1], l1, sub_ref, (lq1_ref, lk1_ref, lq2_ref, lk2_ref), o_ref, lam_init)


def _attn_online_kernel(q_ref, k_ref, vt_ref, sub_ref, lq1_ref, lk1_ref, lq2_ref, lk2_ref, o_ref,
                        m_scr, l_scr, acc_scr, *, blk, lam_init):
    qi = pl.program_id(2)
    qc = _split_components(q_ref[...])
    m_scr[...] = jnp.full(m_scr.shape, NEG, F32)
    l_scr[...] = jnp.zeros(l_scr.shape, F32)
    acc_scr[...] = jnp.zeros(acc_scr.shape, F32)

    def update(j, masked):
        kb = k_ref[pl.ds(pl.multiple_of(j * blk, blk), blk), :]
        vb = vt_ref[j]
        for c in range(2):
            s = _dot_nt(kb, qc[c])
            if masked:
                s = jnp.where(_causal_keep(s.shape), s, NEG)
            m_old = m_scr[c]
            m_new = jnp.maximum(m_old, jnp.max(s, axis=0, keepdims=True))
            alpha = jnp.exp2(m_old - m_new)
            p = jnp.exp2(s - m_new)
            l_scr[c] = alpha * l_scr[c] + jnp.sum(p, axis=0, keepdims=True)
            acc_scr[c] = alpha * acc_scr[c] + _dot(vb, p.astype(BF16))
            m_scr[c] = m_new

    def body(j, carry):
        update(j, masked=False)
        return carry

    lax.fori_loop(0, qi, body, 0)
    update(qi, masked=True)
    _attn_finish(acc_scr[0], l_scr[0], acc_scr[1], l_scr[1], sub_ref, (lq1_ref, lk1_ref, lq2_ref, lk2_ref),
                 o_ref, lam_init)


def _attn_prompt_online(q, k, vt, sub, lams, batch, seq, lam_init):
    m, d = q.shape
    blk = ATTN_BLOCK
    nb = seq // blk
    small = _const_spec((1, HEAD_DIM))
    return pl.pallas_call(
        functools.partial(_attn_online_kernel, blk=blk, lam_init=lam_init),
        grid=(batch, N_HEADS, nb),
        in_specs=[pl.BlockSpec((blk, V_DIM), lambda b, h, i: (b * nb + i, h)),
                  pl.BlockSpec((seq, V_DIM), lambda b, h, i: (b, h)),
                  pl.BlockSpec((None, nb, V_DIM, blk), lambda b, h, i: (b, 0, h, 0)),
                  _const_spec((1, V_DIM)), small, small, small, small],
        out_specs=pl.BlockSpec((blk, V_DIM), lambda b, h, i: (b * nb + i, h)),
        out_shape=jax.ShapeDtypeStruct((m, d), BF16),
        scratch_shapes=[pltpu.VMEM((2, 1, blk), F32), pltpu.VMEM((2, 1, blk), F32),
                        pltpu.VMEM((2, V_DIM, blk), F32)],
        compiler_params=_params("parallel", "parallel", "arbitrary"),
        name="attn_prompt_online",
    )(q, k, vt, sub, *lams)


def _attn_prompt_with_decode(q, k, vt, sub, lams, q_s, k_new, v_new, cache_kt, cache_v, page_table,
                             batch, seq, lam_init):
    m, d = q.shape
    blk = ATTN_BLOCK
    nb = seq // blk
    dec_b = q_s.shape[0]
    n_pages = page_table.shape[1]
    pps = PAGES_PER_STEP
    nbuf = FUSED_DECODE_BUFFERS
    groups = batch * N_HEADS
    assert n_pages % pps == 0 and (dec_b * (n_pages // pps)) % groups == 0
    steps = n_pages // pps

    def whole(shape):
        nd = len(shape)
        return pl.BlockSpec(shape, lambda b, h, i, pt: (0,) * nd)

    kern = functools.partial(
        _attn_bounded_kernel, blk=blk, lam_init=lam_init, chunks_per_group=dec_b * steps // groups,
        decode_cfg=dict(pps=pps, steps=steps, nbuf=nbuf))
    stage = pltpu.VMEM((2, blk // 2, blk), F32)
    grid_spec = pltpu.PrefetchScalarGridSpec(
        num_scalar_prefetch=1,
        grid=(batch, N_HEADS, nb),
        in_specs=[pl.BlockSpec((blk, V_DIM), lambda b, h, i, pt: (b * nb + i, h)),
                  pl.BlockSpec((seq, V_DIM), lambda b, h, i, pt: (b, h)),
                  pl.BlockSpec((None, nb, V_DIM, blk), lambda b, h, i, pt: (b, 0, h, 0)),
                  whole((1, V_DIM))] + [whole((1, HEAD_DIM))] * 4
        + [whole(q_s.shape), whole(k_new.shape), whole(v_new.shape),
           pl.BlockSpec(memory_space=pl.ANY), pl.BlockSpec(memory_space=pl.ANY)],
        out_specs=[pl.BlockSpec((blk, V_DIM), lambda b, h, i, pt: (b * nb + i, h)),
                   whole((dec_b, N_HEADS, V_DIM))],
        scratch_shapes=[stage, stage, pltpu.VMEM((2, 8, blk), F32), pltpu.VMEM((2, V_DIM, blk), F32)]
        + _decode_scratch(nbuf, pps),
    )
    return pl.pallas_call(
        kern,
        grid_spec=grid_spec,
        out_shape=[jax.ShapeDtypeStruct((m, d), BF16), jax.ShapeDtypeStruct((dec_b, N_HEADS, V_DIM), F32)],
        compiler_params=_params("arbitrary", "arbitrary", "arbitrary"),
        name="attn_prompt_decode",
    )(page_table, q, k, vt, sub, *lams, q_s, k_new, v_new, cache_kt, cache_v)


def _decode_scratch(nbuf, pps):
    d = D_MODEL
    nhc = 2 * N_HEADS
    return [pltpu.VMEM((nbuf, pps, d, PAGE_SIZE), F32),
            pltpu.VMEM((nbuf, pps, PAGE_SIZE * N_HEADS, V_DIM), F32),
            pltpu.SemaphoreType.DMA((nbuf,)),
            pltpu.VMEM((d, pps * PAGE_SIZE), BF16),
            pltpu.VMEM((N_HEADS // 2, pps * PAGE_SIZE, 2 * V_DIM), BF16),
            pltpu.VMEM((nhc, d), BF16), pltpu.VMEM((nhc, 1), F32),
            pltpu.VMEM((nhc, 1), F32), pltpu.VMEM((N_HEADS // 2, nhc, 2 * V_DIM), F32)]


def _decode_program(pt_ref, q_ref, kn_ref, vn_ref, sub_ref, lq1_ref, lk1_ref, lq2_ref, lk2_ref,
                    kt_hbm, v_hbm, o_ref, kbuf, vbuf, sem, kb_scr, vb_scr, qbd_scr, m_scr, l_scr, acc_scr,
                    *, pps, steps, nbuf, lam_init):
    nhc = 2 * N_HEADS
    total = q_ref.shape[0] * steps

    def chunk_copies(t, slot):
        b = t // steps
        c = t % steps
        copies = []
        for j in range(pps):
            page = pt_ref[b, c * pps + j]
            copies.append(pltpu.make_async_copy(kt_hbm.at[page], kbuf.at[slot, j], sem.at[slot]))
            copies.append(pltpu.make_async_copy(v_hbm.at[page], vbuf.at[slot, j], sem.at[slot]))
        return copies

    def init(b):
        row = lax.broadcasted_iota(jnp.int32, (nhc, D_MODEL), 0)
        col = lax.broadcasted_iota(jnp.int32, (nhc, D_MODEL), 1)
        q = jnp.broadcast_to(q_ref[b].astype(F32), (nhc, D_MODEL))
        qbd = jnp.where(col // HEAD_DIM == row, q, 0.0)
        qbd_scr[...] = qbd.astype(BF16)
        m_scr[...] = jnp.sum(qbd * kn_ref[b], axis=1, keepdims=True)
        l_scr[...] = jnp.ones(l_scr.shape, F32)
        vn = jnp.broadcast_to(vn_ref[b], (nhc, D_MODEL))
        for hp in range(N_HEADS // 2):
            acc_scr[hp] = vn[:, hp * 2 * V_DIM:(hp + 1) * 2 * V_DIM]

    def accumulate(slot):
        for j in range(pps):
            tok = slice(j * PAGE_SIZE, (j + 1) * PAGE_SIZE)
            kb_scr[:, tok] = kbuf[slot, j].astype(BF16)
            for h in range(N_HEADS):
                lanes = slice((h % 2) * V_DIM, (h % 2 + 1) * V_DIM)
                vb_scr[h // 2, tok, lanes] = vbuf[slot, j, pl.ds(h, PAGE_SIZE, stride=N_HEADS), :].astype(BF16)
        s = _dot(qbd_scr[...], kb_scr[...])
        m_old = m_scr[...]
        m_new = jnp.maximum(m_old, jnp.max(s, axis=1, keepdims=True))
        alpha = jnp.exp2(m_old - m_new)
        p = jnp.exp2(s - m_new)
        l_scr[...] = alpha * l_scr[...] + jnp.sum(p, axis=1, keepdims=True)
        pb = p.astype(BF16)
        for hp in range(N_HEADS // 2):
            acc_scr[hp] = alpha * acc_scr[hp] + _dot(pb, vb_scr[hp])
        m_scr[...] = m_new

    def finish(b):
        lam = _diff_lambda(lq1_ref[...], lk1_ref[...], lq2_ref[...], lk2_ref[...], lam_init)
        row = lax.broadcasted_iota(jnp.int32, (nhc, V_DIM), 0)
        inv_l = 1.0 / l_scr[...]
        outs = []
        for h in range(N_HEADS):
            a = acc_scr[h // 2][:, (h % 2) * V_DIM:(h % 2 + 1) * V_DIM] * inv_l
            a = jnp.where(row == 2 * h, a, 0.0) - lam * jnp.where(row == 2 * h + 1, a, 0.0)
            outs.append(jnp.sum(a, axis=0, keepdims=True))
        o = jnp.concatenate(outs, axis=0)
        ms = jnp.mean(o * o, axis=1, keepdims=True)
        o_ref[b] = ((o * lax.rsqrt(ms + EPS)) * sub_ref[...]) * (1.0 - lam_init)

    def prologue():
        for t0 in range(nbuf - 1):
            for cp in chunk_copies(t0, t0):
                cp.start()

    def process(t):
        slot = lax.rem(t, nbuf)
        ahead = t + (nbuf - 1)

        @pl.when(ahead < total)
        def _prefetch():
            for cp in chunk_copies(ahead, lax.rem(ahead, nbuf)):
                cp.start()

        for cp in chunk_copies(t, slot):
            cp.wait()
        b = t // steps
        c = t % steps
        pl.when(c == 0)(lambda: init(b))
        accumulate(slot)
        pl.when(c == steps - 1)(lambda: finish(b))

    return prologue, process


def _decode_kernel(*refs, **cfg):
    prologue, process = _decode_program(*refs, **cfg)
    t = pl.program_id(0)
    pl.when(t == 0)(prologue)
    process(t)


def _attn_sample(q, k_new, v_new, cache_kt, cache_v, page_table, sub, lams, lam_init):
    b = q.shape[0]
    d = D_MODEL
    n_pages = page_table.shape[1]
    pps = PAGES_PER_STEP
    nbuf = DECODE_BUFFERS
    assert n_pages % pps == 0
    steps = n_pages // pps
    nhc = 2 * N_HEADS

    def whole(shape):
        nd = len(shape)
        return pl.BlockSpec(shape, lambda i, pt: (0,) * nd)

    kern = functools.partial(_decode_kernel, pps=pps, steps=steps, nbuf=nbuf, lam_init=lam_init)
    grid_spec = pltpu.PrefetchScalarGridSpec(
        num_scalar_prefetch=1,
        grid=(b * steps,),
        in_specs=[whole(q.shape), whole(k_new.shape), whole(v_new.shape), whole((1, V_DIM))]
        + [whole((1, HEAD_DIM))] * 4
        + [pl.BlockSpec(memory_space=pl.ANY), pl.BlockSpec(memory_space=pl.ANY)],
        out_specs=whole((b, N_HEADS, V_DIM)),
        scratch_shapes=_decode_scratch(nbuf, pps),
    )
    return pl.pallas_call(
        kern,
        grid_spec=grid_spec,
        out_shape=jax.ShapeDtypeStruct((b, N_HEADS, V_DIM), F32),
        compiler_params=_params("arbitrary"),
        name="attn_sample",
    )(page_table, q, k_new, v_new, sub, *lams, cache_kt, cache_v)


def _post_kernel(x_ref, a_ref, wmix_ref, g_ref, wup_ref, wdown_ref, o_ref):
    x1 = x_ref[...] + _dot(a_ref[...], wmix_ref[...])
    h = _rms(x1, g_ref[...]).astype(BF16)
    acc = x1
    ff = wup_ref.shape[1]
    for c in range(ff // FF_CHUNK):
        sl = slice(c * FF_CHUNK, (c + 1) * FF_CHUNK)
        u = jnp.maximum(_dot(h, wup_ref[:, sl]), 0.0)
        acc = acc + _dot((u * u).astype(BF16), wdown_ref[sl, :])
    o_ref[...] = acc


def _post(x, a, w_mix, g, w_up, w_down, tm):
    m, d = x.shape
    ff = w_up.shape[1]
    row = pl.BlockSpec((tm, d), lambda i: (i, 0))
    return pl.pallas_call(
        _post_kernel,
        grid=(m // tm,),
        in_specs=[row, row, _const_spec((d, d)), _const_spec((1, d)), _const_spec((d, ff)),
                  _const_spec((ff, d))],
        out_specs=row,
        out_shape=jax.ShapeDtypeStruct((m, d), F32),
        compiler_params=_params("parallel"),
        name="post",
    )(x, a, w_mix, g, w_up, w_down)


def _conv_gates(x, g, w_ref):
    d = D_MODEL
    h = _rms(x, g).astype(BF16)
    g_b = _dot(h, w_ref[:, 0:d])
    u = _dot(h, w_ref[:, d:2 * d]) * _dot(h, w_ref[:, 2 * d:3 * d])
    return g_b, u


def _conv_prompt_kernel(x_ref, g_ref, w_ref, cw_ref, a_ref, st_ref, tail_scr, *, seq_blocks):
    i = pl.program_id(0)

    @pl.when(i % seq_blocks == 0)
    def _start_of_sequence():
        tail_scr[...] = jnp.zeros(tail_scr.shape, F32)

    g_b, u = _conv_gates(x_ref[...], g_ref[...], w_ref)
    tm = u.shape[0]
    rows = lax.broadcasted_iota(jnp.int32, u.shape, 0)
    t0 = tail_scr[0:1, :]
    t1 = tail_scr[1:2, :]
    um1 = jnp.where(rows == 0, t1, pltpu.roll(u, 1, axis=0))
    um2 = jnp.where(rows == 0, t0, jnp.where(rows == 1, t1, pltpu.roll(u, 2, axis=0)))
    cw = cw_ref[...]
    conv = cw[0:1, :] * um2 + cw[1:2, :] * um1 + cw[2:3, :] * u
    a_ref[...] = (g_b * conv).astype(BF16)
    last = u[tm - 2:tm, :]
    tail_scr[0:2, :] = last
    st_ref[...] = last


def _conv_prompt(x, g, w_in, conv_w, batch, seq, tm):
    m, d = x.shape
    nb = seq // tm
    row = pl.BlockSpec((tm, d), lambda i: (i, 0))
    kern = functools.partial(_conv_prompt_kernel, seq_blocks=nb)
    return pl.pallas_call(
        kern,
        grid=(m // tm,),
        in_specs=[row, _const_spec((1, d)), _const_spec((d, 3 * d)), _const_spec(conv_w.shape)],
        out_specs=[row, pl.BlockSpec((None, 2, d), lambda i: (i // nb, 0, 0))],
        out_shape=[jax.ShapeDtypeStruct((m, d), BF16), jax.ShapeDtypeStruct((batch, 2, d), F32)],
        scratch_shapes=[pltpu.VMEM((8, d), F32)],
        compiler_params=_params("arbitrary"),
        name="conv_prompt",
    )(x, g, w_in, conv_w)


def _conv_sample_kernel(x_ref, s0_ref, s1_ref, g_ref, w_ref, cw_ref, a_ref, u_ref):
    g_b, u = _conv_gates(x_ref[...], g_ref[...], w_ref)
    cw = cw_ref[...]
    conv = cw[0:1, :] * s0_ref[...] + cw[1:2, :] * s1_ref[...] + cw[2:3, :] * u
    a_ref[...] = (g_b * conv).astype(BF16)
    u_ref[...] = u


def _conv_sample(x, s0, s1, g, w_in, conv_w):
    m, d = x.shape
    full = _const_spec((m, d))
    return pl.pallas_call(
        _conv_sample_kernel,
        grid=(1,),
        in_specs=[full, full, full, _const_spec((1, d)), _const_spec((d, 3 * d)), _const_spec(conv_w.shape)],
        out_specs=[full, full],
        out_shape=[jax.ShapeDtypeStruct((m, d), BF16), jax.ShapeDtypeStruct((m, d), F32)],
        compiler_params=_params("arbitrary"),
        name="conv_sample",
    )(x, s0, s1, g, w_in, conv_w)


def kernel(x_prompt, x_sample, cache_k, cache_v, state_conv, page_table, norm_mix, w_qkv, q_gain, k_gain,
           lambda_q1, lambda_k1, lambda_q2, lambda_k2, subln, w_o, w_in, conv_w, w_out, norm_mlp, w_up, w_down):
    batch, seq, d = x_prompt.shape
    dec_batch, dec_seq, _ = x_sample.shape
    assert d == D_MODEL and dec_seq == 1 and seq % ATTN_BLOCK == 0 and ROW_BLOCK == ATTN_BLOCK
    n_pool = cache_k.shape[1]
    m_p = batch * seq
    m_s = dec_batch * dec_seq
    tm = ROW_BLOCK

    xp = x_prompt.reshape(m_p, d)
    xs = x_sample.reshape(m_s, d)

    lam_init = _lambda_init(0)
    g0 = norm_mix[0].reshape(1, d)
    wqkv = w_qkv[0].astype(BF16)
    reps = d // HEAD_DIM
    qg = (jnp.tile(q_gain[0], reps) * (SCALE * LOG2E)).reshape(1, d)
    kg = jnp.tile(k_gain[0], reps).reshape(1, d)
    idx = jnp.arange(V7X_MXU_DIM) // HEAD_DIM
    gmat = (idx[:, None] == idx[None, :]).astype(BF16)
    lams = [a[0].reshape(1, HEAD_DIM) for a in (lambda_q1, lambda_k1, lambda_q2, lambda_k2)]
    sub = subln[0].reshape(1, V_DIM)

    q_p, kb_p, kt_p, vf_p, vt_p = _qkv_prompt(xp, g0, wqkv, qg, kg, gmat, tm, seq // tm)
    score_bound = HEAD_DIM * SCALE * LOG2E * jnp.max(jnp.abs(q_gain[0])) * jnp.max(jnp.abs(k_gain[0]))

    q_s, kf_s, vf_s = _qkv_sample(xs, g0, wqkv, qg, kg, gmat)
    cache_kt = jnp.transpose(cache_k[0], (0, 2, 3, 4, 1)).reshape(n_pool, d, PAGE_SIZE)
    cache_vr = cache_v[0].reshape(n_pool, PAGE_SIZE * N_HEADS, V_DIM)

    def attend_bounded(q, k, vt, sb, lm, qs, kn, vn, ckt, cvr, pt):
        return tuple(_attn_prompt_with_decode(q, k, vt, sb, lm, qs, kn, vn, ckt, cvr, pt, batch, seq, lam_init))

    def attend_any(q, k, vt, sb, lm, qs, kn, vn, ckt, cvr, pt):
        return (_attn_prompt_online(q, k, vt, sb, lm, batch, seq, lam_init),
                _attn_sample(qs, kn, vn, ckt, cvr, pt, sb, lm, lam_init))

    o_p, o_s = lax.cond(
        score_bound <= MAX_UNSHIFTED_LOG2_SCORE, attend_bounded, attend_any,
        q_p, kb_p, vt_p, sub, lams, q_s.reshape(m_s, 1, d), kf_s.reshape(m_s, 1, d), vf_s.reshape(m_s, 1, d),
        cache_kt, cache_vr, page_table)
    o_s = o_s.reshape(m_s, d).astype(BF16)

    wo = w_o[0].astype(BF16)
    gm0 = norm_mlp[0].reshape(1, d)
    wup0 = w_up[0].astype(BF16)
    wdown0 = w_down[0].astype(BF16)
    xp = _post(xp, o_p, wo, gm0, wup0, wdown0, tm)
    xs = _post(xs, o_s, wo, gm0, wup0, wdown0, m_s)

    g1 = norm_mix[1].reshape(1, d)
    win = w_in[0].astype(BF16)
    a_p, conv_p = _conv_prompt(xp, g1, win, conv_w[0], batch, seq, tm)
    a_s, u_s = _conv_sample(xs, state_conv[0, :, 0, :], state_conv[0, :, 1, :], g1, win, conv_w[0])

    wout = w_out[0].astype(BF16)
    gm1 = norm_mlp[1].reshape(1, d)
    wup1 = w_up[1].astype(BF16)
    wdown1 = w_down[1].astype(BF16)
    xp = _post(xp, a_p, wout, gm1, wup1, wdown1, tm)
    xs = _post(xs, a_s, wout, gm1, wup1, wdown1, m_s)

    conv_s = jnp.stack([state_conv[0, :, 1, :], u_s], axis=1)

    return (xp.reshape(batch, seq, d), xs.reshape(dec_batch, dec_seq, d),
            kt_p.reshape(1, batch, N_HEADS, 2, HEAD_DIM, seq).transpose(0, 1, 5, 2, 3, 4),
            vf_p.reshape(1, batch, seq, N_HEADS, V_DIM),
            kf_s.reshape(1, dec_batch, dec_seq, N_HEADS, 2, HEAD_DIM),
            vf_s.reshape(1, dec_batch, dec_seq, N_HEADS, V_DIM),
            conv_p.reshape(1, batch, 2, d), conv_s.reshape(1, dec_batch, 2, d))
```

```python
import functools
import math

import jax
import jax.numpy as jnp
from jax import lax
from jax.experimental import pallas as pl
from jax.experimental.pallas import tpu as pltpu

F32 = jnp.float32
BF16 = jnp.bfloat16

N_HEADS = 8
HEAD_DIM = 64
V_DIM = 2 * HEAD_DIM
D_MODEL = N_HEADS * V_DIM
EPS = 1e-6
SCALE = HEAD_DIM ** -0.5
NEG = -1e30
LOG2E = math.log2(math.e)
MAX_UNSHIFTED_LOG2_SCORE = 64.0
PAGE_SIZE = 128

V7X_MXU_DIM = 256
V7X_VMEM_BYTES = 64 * 1024 * 1024
V7X_VMEM_LIMIT_BYTES = V7X_VMEM_BYTES - 2 * 1024 * 1024

ROW_BLOCK = 512
ATTN_BLOCK = 512
PAGES_PER_STEP = 16
STAGE_PAGES = 8
DECODE_BUFFERS = 2
FUSED_DECODE_BUFFERS = 3
ATTN_FIXED_WORK = 1.7
FF_CHUNK = 1024


def _lambda_init(layer):
    return 0.8 - 0.6 * math.exp(-0.3 * layer)


def _rms(x, g):
    ms = jnp.mean(x * x, axis=-1, keepdims=True)
    return x * lax.rsqrt(ms + EPS) * g


def _dot(a, b):
    return jnp.dot(a, b, preferred_element_type=F32)


def _dot_nt(a, b):
    return lax.dot_general(a, b, (((1,), (1,)), ((), ())), preferred_element_type=F32)


def _const_spec(shape):
    nd = len(shape)
    return pl.BlockSpec(shape, lambda *_: (0,) * nd, pipeline_mode=pl.Buffered(1))


def _params(*sem):
    return pltpu.CompilerParams(dimension_semantics=sem, vmem_limit_bytes=V7X_VMEM_LIMIT_BYTES)


def _group_mean_square(t, gmat):
    sq = t * t
    hi = sq.astype(BF16)
    lo = (sq - hi.astype(F32)).astype(BF16)
    parts = []
    for c in range(t.shape[1] // V7X_MXU_DIM):
        sl = slice(c * V7X_MXU_DIM, (c + 1) * V7X_MXU_DIM)
        parts.append(_dot(hi[:, sl], gmat) + _dot(lo[:, sl], gmat))
    return jnp.concatenate(parts, axis=1) * (1.0 / HEAD_DIM)


def _qkv_values(x_ref, g_ref, w_ref, qg_ref, kg_ref, gmat_ref):
    d = D_MODEL
    h = _rms(x_ref[...], g_ref[...]).astype(BF16)
    gmat = gmat_ref[...]
    tq = _dot(h, w_ref[:, 0:d])
    qn = tq * lax.rsqrt(_group_mean_square(tq, gmat) + EPS) * qg_ref[...]
    tk = _dot(h, w_ref[:, d:2 * d])
    kn = tk * lax.rsqrt(_group_mean_square(tk, gmat) + EPS) * kg_ref[...]
    tv = _dot(h, w_ref[:, 2 * d:3 * d])
    return qn, kn, tv


def _qkv_prompt_kernel(x_ref, g_ref, w_ref, qg_ref, kg_ref, gmat_ref, q_ref, kb_ref, kt_ref, vf_ref, vt_ref):
    qn, kn, tv = _qkv_values(x_ref, g_ref, w_ref, qg_ref, kg_ref, gmat_ref)
    q_ref[...] = qn.astype(BF16)
    kb_ref[...] = kn.astype(BF16)
    kt_ref[...] = kn.T
    vf_ref[...] = tv
    vt_ref[...] = tv.T.astype(BF16)


def _qkv_sample_kernel(x_ref, g_ref, w_ref, qg_ref, kg_ref, gmat_ref, q_ref, kf_ref, vf_ref):
    qn, kn, tv = _qkv_values(x_ref, g_ref, w_ref, qg_ref, kg_ref, gmat_ref)
    q_ref[...] = qn.astype(BF16)
    kf_ref[...] = kn
    vf_ref[...] = tv


def _qkv_in_specs(tm, d):
    return [pl.BlockSpec((tm, d), lambda i: (i, 0)), _const_spec((1, d)), _const_spec((d, 3 * d)),
            _const_spec((1, d)), _const_spec((1, d)), _const_spec((V7X_MXU_DIM, V7X_MXU_DIM))]


def _qkv_prompt(x, g, w, qg, kg, gmat, tm, nb):
    m, d = x.shape
    batch = m // (tm * nb)
    row = pl.BlockSpec((tm, d), lambda i: (i, 0))
    return pl.pallas_call(
        _qkv_prompt_kernel,
        grid=(m // tm,),
        in_specs=_qkv_in_specs(tm, d),
        out_specs=[row, row, pl.BlockSpec((None, d, tm), lambda i: (i // nb, 0, i % nb)), row,
                   pl.BlockSpec((None, None, d, tm), lambda i: (i // nb, i % nb, 0, 0))],
        out_shape=[jax.ShapeDtypeStruct((m, d), BF16), jax.ShapeDtypeStruct((m, d), BF16),
                   jax.ShapeDtypeStruct((batch, d, tm * nb), F32), jax.ShapeDtypeStruct((m, d), F32),
                   jax.ShapeDtypeStruct((batch, nb, d, tm), BF16)],
        compiler_params=_params("parallel"),
        name="qkv_prompt",
    )(x, g, w, qg, kg, gmat)


def _qkv_sample(x, g, w, qg, kg, gmat):
    m, d = x.shape
    row = pl.BlockSpec((m, d), lambda i: (i, 0))
    return pl.pallas_call(
        _qkv_sample_kernel,
        grid=(1,),
        in_specs=_qkv_in_specs(m, d),
        out_specs=[row, row, row],
        out_shape=[jax.ShapeDtypeStruct((m, d), BF16), jax.ShapeDtypeStruct((m, d), F32),
                   jax.ShapeDtypeStruct((m, d), F32)],
        compiler_params=_params("parallel"),
        name="qkv_sample",
    )(x, g, w, qg, kg, gmat)


def _diff_lambda(lq1, lk1, lq2, lk2, lam_init):
    d1 = jnp.sum(lq1 * lk1, axis=-1, keepdims=True)
    d2 = jnp.sum(lq2 * lk2, axis=-1, keepdims=True)
    return jnp.exp(d1) - jnp.exp(d2) + lam_init


def _decode_scratch(nbuf, pps):
    d = D_MODEL
    nhc = 2 * N_HEADS
    return [pltpu.VMEM((nbuf, pps, d, PAGE_SIZE), F32),
            pltpu.VMEM((nbuf, pps, PAGE_SIZE * N_HEADS, V_DIM), F32),
            pltpu.SemaphoreType.DMA((nbuf,)),
            pltpu.VMEM((d, min(pps, STAGE_PAGES) * PAGE_SIZE), BF16),
            pltpu.VMEM((N_HEADS // 2, min(pps, STAGE_PAGES) * PAGE_SIZE, 2 * V_DIM), BF16),
            pltpu.VMEM((nhc, d), BF16), pltpu.VMEM((nhc, 1), F32),
            pltpu.VMEM((nhc, 1), F32), pltpu.VMEM((N_HEADS // 2, nhc, 2 * V_DIM), F32)]


def _decode_program(pt_ref, q_ref, kn_ref, vn_ref, sub_ref, lq1_ref, lk1_ref, lq2_ref, lk2_ref,
                    kt_hbm, v_hbm, o_ref, kbuf, vbuf, sem, kb_scr, vb_scr, qbd_scr, m_scr, l_scr, acc_scr,
                    *, pps, steps, nbuf, lam_init):
    nhc = 2 * N_HEADS
    total = q_ref.shape[0] * steps

    def chunk_copies(t, slot):
        b = t // steps
        c = t % steps
        copies = []
        for j in range(pps):
            page = pt_ref[b, c * pps + j]
            copies.append(pltpu.make_async_copy(kt_hbm.at[page], kbuf.at[slot, j], sem.at[slot]))
            copies.append(pltpu.make_async_copy(v_hbm.at[page], vbuf.at[slot, j], sem.at[slot]))
        return copies

    def init(b):
        row = lax.broadcasted_iota(jnp.int32, (nhc, D_MODEL), 0)
        col = lax.broadcasted_iota(jnp.int32, (nhc, D_MODEL), 1)
        q = jnp.broadcast_to(q_ref[b].astype(F32), (nhc, D_MODEL))
        qbd = jnp.where(col // HEAD_DIM == row, q, 0.0)
        qbd_scr[...] = qbd.astype(BF16)
        m_scr[...] = jnp.sum(qbd * kn_ref[b], axis=1, keepdims=True)
        l_scr[...] = jnp.ones(l_scr.shape, F32)
        vn = jnp.broadcast_to(vn_ref[b], (nhc, D_MODEL))
        for hp in range(N_HEADS // 2):
            acc_scr[hp] = vn[:, hp * 2 * V_DIM:(hp + 1) * 2 * V_DIM]

    def accumulate(slot):
        groups = [range(g, min(g + STAGE_PAGES, pps)) for g in range(0, pps, STAGE_PAGES)]
        qbd = qbd_scr[...]
        parts = []
        for pages in groups:
            for i, j in enumerate(pages):
                kb_scr[:, i * PAGE_SIZE:(i + 1) * PAGE_SIZE] = kbuf[slot, j].astype(BF16)
            parts.append(_dot(qbd, kb_scr[:, :len(pages) * PAGE_SIZE]))
        s = jnp.concatenate(parts, axis=1)
        m_old = m_scr[...]
        m_new = jnp.maximum(m_old, jnp.max(s, axis=1, keepdims=True))
        alpha = jnp.exp2(m_old - m_new)
        p = jnp.exp2(s - m_new)
        l_scr[...] = alpha * l_scr[...] + jnp.sum(p, axis=1, keepdims=True)
        pb = p.astype(BF16)
        m_scr[...] = m_new
        for hp in range(N_HEADS // 2):
            acc_scr[hp] = alpha * acc_scr[hp]
        for pages in groups:
            n_tok = len(pages) * PAGE_SIZE
            for i, j in enumerate(pages):
                tok = slice(i * PAGE_SIZE, (i + 1) * PAGE_SIZE)
                for h in range(N_HEADS):
                    lanes = slice((h % 2) * V_DIM, (h % 2 + 1) * V_DIM)
                    vb_scr[h // 2, tok, lanes] = vbuf[slot, j, pl.ds(h, PAGE_SIZE, stride=N_HEADS), :].astype(BF16)
            pg = pb[:, pages[0] * PAGE_SIZE:pages[0] * PAGE_SIZE + n_tok]
            for hp in range(N_HEADS // 2):
                acc_scr[hp] += _dot(pg, vb_scr[hp, :n_tok, :])

    def finish(b):
        lam = _diff_lambda(lq1_ref[...], lk1_ref[...], lq2_ref[...], lk2_ref[...], lam_init)
        row = lax.broadcasted_iota(jnp.int32, (nhc, V_DIM), 0)
        inv_l = 1.0 / l_scr[...]
        outs = []
        for h in range(N_HEADS):
            a = acc_scr[h // 2][:, (h % 2) * V_DIM:(h % 2 + 1) * V_DIM] * inv_l
            a = jnp.where(row == 2 * h, a, 0.0) - lam * jnp.where(row == 2 * h + 1, a, 0.0)
            outs.append(jnp.sum(a, axis=0, keepdims=True))
        o = jnp.concatenate(outs, axis=0)
        ms = jnp.mean(o * o, axis=1, keepdims=True)
        o_ref[b] = ((o * lax.rsqrt(ms + EPS)) * sub_ref[...]) * (1.0 - lam_init)

    def prologue():
        for t0 in range(nbuf - 1):
            for cp in chunk_copies(t0, t0):
                cp.start()

    def process(t):
        slot = lax.rem(t, nbuf)
        ahead = t + (nbuf - 1)

        @pl.when(ahead < total)
        def _prefetch():
            for cp in chunk_copies(ahead, lax.rem(ahead, nbuf)):
                cp.start()

        for cp in chunk_copies(t, slot):
            cp.wait()
        b = t // steps
        c = t % steps
        pl.when(c == 0)(lambda: init(b))
        accumulate(slot)
        pl.when(c == steps - 1)(lambda: finish(b))

    return prologue, process


def _decode_kernel(*refs, **cfg):
    prologue, process = _decode_program(*refs, **cfg)
    t = pl.program_id(0)
    pl.when(t == 0)(prologue)
    process(t)


def _attn_sample(q, k_new, v_new, cache_kt, cache_v, page_table, sub, lams, lam_init):
    b = q.shape[0]
    n_pages = page_table.shape[1]
    pps = PAGES_PER_STEP
    nbuf = DECODE_BUFFERS
    assert n_pages % pps == 0
    steps = n_pages // pps

    def whole(shape):
        nd = len(shape)
        return pl.BlockSpec(shape, lambda i, pt: (0,) * nd)

    kern = functools.partial(_decode_kernel, pps=pps, steps=steps, nbuf=nbuf, lam_init=lam_init)
    grid_spec = pltpu.PrefetchScalarGridSpec(
        num_scalar_prefetch=1,
        grid=(b * steps,),
        in_specs=[whole(q.shape), whole(k_new.shape), whole(v_new.shape), whole((1, V_DIM))]
        + [whole((1, HEAD_DIM))] * 4
        + [pl.BlockSpec(memory_space=pl.ANY), pl.BlockSpec(memory_space=pl.ANY)],
        out_specs=whole((b, N_HEADS, V_DIM)),
        scratch_shapes=_decode_scratch(nbuf, pps),
    )
    return pl.pallas_call(
        kern,
        grid_spec=grid_spec,
        out_shape=jax.ShapeDtypeStruct((b, N_HEADS, V_DIM), F32),
        compiler_params=_params("arbitrary"),
        name="attn_sample",
    )(page_table, q, k_new, v_new, sub, *lams, cache_kt, cache_v)


def _split_components(q):
    lane = lax.broadcasted_iota(jnp.int32, q.shape, 1)
    zero = jnp.zeros_like(q)
    return jnp.where(lane < HEAD_DIM, q, zero), jnp.where(lane >= HEAD_DIM, q, zero)


def _causal_keep(shape):
    kpos = lax.broadcasted_iota(jnp.int32, shape, 0)
    qpos = lax.broadcasted_iota(jnp.int32, shape, 1)
    return kpos <= qpos


def _attn_finish(acc0, l0, acc1, l1, sub_ref, lam_refs, o_ref, lam_init):
    lam = _diff_lambda(*(r[...] for r in lam_refs), lam_init)
    o_t = acc0 / l0 - lam * (acc1 / l1)
    ms = jnp.mean(o_t * o_t, axis=0, keepdims=True)
    o_n = (o_t * lax.rsqrt(ms + EPS)).T
    o_ref[...] = ((o_n * sub_ref[...]) * (1.0 - lam_init)).astype(BF16)


def _chunks_per_query_block(nb, chunks):
    weights = [ATTN_FIXED_WORK + i for i in range(nb)]
    ideal = [chunks * w / sum(weights) for w in weights]
    counts = [int(x) for x in ideal]
    by_remainder = sorted(range(nb), key=lambda i: ideal[i] - counts[i], reverse=True)
    for i in by_remainder[:chunks - sum(counts)]:
        counts[i] += 1
    return counts


def _attn_bounded_kernel(pt_ref, q_ref, k_ref, vt_ref, sub_ref, lq1_ref, lk1_ref, lq2_ref, lk2_ref,
                         qs_ref, kn_ref, vn_ref, kt_hbm, v_hbm, o_ref, os_ref,
                         s_even, s_odd, l_scr, acc_scr, *decode_scratch,
                         blk, lam_init, chunks_per_group, decode_cfg):
    half = blk // 2
    qi = pl.program_id(2)
    group = pl.program_id(0) * pl.num_programs(1) + pl.program_id(1)

    prologue, process = _decode_program(
        pt_ref, qs_ref, kn_ref, vn_ref, sub_ref, lq1_ref, lk1_ref, lq2_ref, lk2_ref, kt_hbm, v_hbm, os_ref,
        *decode_scratch, lam_init=lam_init, **decode_cfg)
    counts = _chunks_per_query_block(k_ref.shape[0] // blk, chunks_per_group)
    first = group * chunks_per_group + sum(jnp.where(qi > i, n, 0) for i, n in enumerate(counts))
    count = sum(jnp.where(qi == i, n, 0) for i, n in enumerate(counts))
    pl.when(jnp.logical_and(group == 0, qi == 0))(prologue)

    def decode_body(i, carry):
        process(first + i)
        return carry

    lax.fori_loop(0, count, decode_body, 0)

    qc = _split_components(q_ref[...])
    l_scr[...] = jnp.zeros(l_scr.shape, F32)
    acc_scr[...] = jnp.zeros(acc_scr.shape, F32)

    def scores(j, hf, dst):
        kb = k_ref[pl.ds(pl.multiple_of(j * blk + hf * half, half), half), :]
        for c in range(2):
            dst[c] = _dot_nt(kb, qc[c])

    def consume(j, hf, src, masked):
        vb = vt_ref[j, :, hf * half:(hf + 1) * half]
        for c in range(2):
            p = jnp.exp2(src[c])
            if masked:
                kpos = lax.broadcasted_iota(jnp.int32, p.shape, 0) + hf * half
                qpos = lax.broadcasted_iota(jnp.int32, p.shape, 1)
                p = jnp.where(kpos <= qpos, p, 0.0)
            l_scr[c] += jnp.sum(p.reshape(half // 8, 8, blk), axis=0)
            acc_scr[c] += _dot(vb, p.astype(BF16))

    scores(0, 0, s_even)

    def body(j, carry):
        scores(j, 1, s_odd)
        consume(j, 0, s_even, masked=False)
        scores(j + 1, 0, s_even)
        consume(j, 1, s_odd, masked=False)
        return carry

    lax.fori_loop(0, qi, body, 0)
    scores(qi, 1, s_odd)
    consume(qi, 0, s_even, masked=True)
    consume(qi, 1, s_odd, masked=True)
    l0 = jnp.sum(l_scr[0], axis=0, keepdims=True)
    l1 = jnp.sum(l_scr[1], axis=0, keepdims=True)
    _attn_finish(acc_scr[0], l0, acc_scr[1], l1, sub_ref, (lq1_ref, lk1_ref, lq2_ref, lk2_ref), o_ref, lam_init)


def _attn_online_kernel(q_ref, k_ref, vt_ref, sub_ref, lq1_ref, lk1_ref, lq2_ref, lk2_ref, o_ref,
                        m_scr, l_scr, acc_scr, *, blk, lam_init):
    qi = pl.program_id(2)
    qc = _split_components(q_ref[...])
    m_scr[...] = jnp.full(m_scr.shape, NEG, F32)
    l_scr[...] = jnp.zeros(l_scr.shape, F32)
    acc_scr[...] = jnp.zeros(acc_scr.shape, F32)

    def update(j, masked):
        kb = k_ref[pl.ds(pl.multiple_of(j * blk, blk), blk), :]
        vb = vt_ref[j]
        for c in range(2):
            s = _dot_nt(kb, qc[c])
            if masked:
                s = jnp.where(_causal_keep(s.shape), s, NEG)
            m_old = m_scr[c]
            m_new = jnp.maximum(m_old, jnp.max(s, axis=0, keepdims=True))
            alpha = jnp.exp2(m_old - m_new)
            p = jnp.exp2(s - m_new)
            l_scr[c] = alpha * l_scr[c] + jnp.sum(p, axis=0, keepdims=True)
            acc_scr[c] = alpha * acc_scr[c] + _dot(vb, p.astype(BF16))
            m_scr[c] = m_new

    def body(j, carry):
        update(j, masked=False)
        return carry

    lax.fori_loop(0, qi, body, 0)
    update(qi, masked=True)
    _attn_finish(acc_scr[0], l_scr[0], acc_scr[1], l_scr[1], sub_ref, (lq1_ref, lk1_ref, lq2_ref, lk2_ref),
                 o_ref, lam_init)


def _attn_prompt_online(q, k, vt, sub, lams, batch, seq, lam_init):
    m, d = q.shape
    blk = ATTN_BLOCK
    nb = seq // blk
    small = _const_spec((1, HEAD_DIM))
    return pl.pallas_call(
        functools.partial(_attn_online_kernel, blk=blk, lam_init=lam_init),
        grid=(batch, N_HEADS, nb),
        in_specs=[pl.BlockSpec((blk, V_DIM), lambda b, h, i: (b * nb + i, h)),
                  pl.BlockSpec((seq, V_DIM), lambda b, h, i: (b, h)),
                  pl.BlockSpec((None, nb, V_DIM, blk), lambda b, h, i: (b, 0, h, 0)),
                  _const_spec((1, V_DIM)), small, small, small, small],
        out_specs=pl.BlockSpec((blk, V_DIM), lambda b, h, i: (b * nb + i, h)),
        out_shape=jax.ShapeDtypeStruct((m, d), BF16),
        scratch_shapes=[pltpu.VMEM((2, 1, blk), F32), pltpu.VMEM((2, 1, blk), F32),
                        pltpu.VMEM((2, V_DIM, blk), F32)],
        compiler_params=_params("parallel", "parallel", "arbitrary"),
        name="attn_prompt_online",
    )(q, k, vt, sub, *lams)


def _attn_prompt_with_decode(q, k, vt, sub, lams, q_s, k_new, v_new, cache_kt, cache_v, page_table,
                             batch, seq, lam_init):
    m, d = q.shape
    blk = ATTN_BLOCK
    nb = seq // blk
    dec_b = q_s.shape[0]
    n_pages = page_table.shape[1]
    pps = PAGES_PER_STEP
    nbuf = FUSED_DECODE_BUFFERS
    groups = batch * N_HEADS
    assert n_pages % pps == 0 and (dec_b * (n_pages // pps)) % groups == 0
    steps = n_pages // pps

    def whole(shape):
        nd = len(shape)
        return pl.BlockSpec(shape, lambda b, h, i, pt: (0,) * nd)

    kern = functools.partial(
        _attn_bounded_kernel, blk=blk, lam_init=lam_init, chunks_per_group=dec_b * steps // groups,
        decode_cfg=dict(pps=pps, steps=steps, nbuf=nbuf))
    stage = pltpu.VMEM((2, blk // 2, blk), F32)
    grid_spec = pltpu.PrefetchScalarGridSpec(
        num_scalar_prefetch=1,
        grid=(batch, N_HEADS, nb),
        in_specs=[pl.BlockSpec((blk, V_DIM), lambda b, h, i, pt: (b * nb + i, h)),
                  pl.BlockSpec((seq, V_DIM), lambda b, h, i, pt: (b, h)),
                  pl.BlockSpec((None, nb, V_DIM, blk), lambda b, h, i, pt: (b, 0, h, 0)),
                  whole((1, V_DIM))] + [whole((1, HEAD_DIM))] * 4
        + [whole(q_s.shape), whole(k_new.shape), whole(v_new.shape),
           pl.BlockSpec(memory_space=pl.ANY), pl.BlockSpec(memory_space=pl.ANY)],
        out_specs=[pl.BlockSpec((blk, V_DIM), lambda b, h, i, pt: (b * nb + i, h)),
                   whole((dec_b, N_HEADS, V_DIM))],
        scratch_shapes=[stage, stage, pltpu.VMEM((2, 8, blk), F32), pltpu.VMEM((2, V_DIM, blk), F32)]
        + _decode_scratch(nbuf, pps),
    )
    return pl.pallas_call(
        kern,
        grid_spec=grid_spec,
        out_shape=[jax.ShapeDtypeStruct((m, d), BF16), jax.ShapeDtypeStruct((dec_b, N_HEADS, V_DIM), F32)],
        compiler_params=_params("arbitrary", "arbitrary", "arbitrary"),
        name="attn_prompt_decode",
    )(page_table, q, k, vt, sub, *lams, q_s, k_new, v_new, cache_kt, cache_v)


def _post_kernel(x_ref, a_ref, wmix_ref, g_ref, wup_ref, wdown_ref, o_ref):
    x1 = x_ref[...] + _dot(a_ref[...], wmix_ref[...])
    h = _rms(x1, g_ref[...]).astype(BF16)
    acc = x1
    ff = wup_ref.shape[1]
    for c in range(ff // FF_CHUNK):
        sl = slice(c * FF_CHUNK, (c + 1) * FF_CHUNK)
        u = jnp.maximum(_dot(h, wup_ref[:, sl]), 0.0)
        acc = acc + _dot((u * u).astype(BF16), wdown_ref[sl, :])
    o_ref[...] = acc


def _post(x, a, w_mix, g, w_up, w_down, tm):
    m, d = x.shape
    ff = w_up.shape[1]
    row = pl.BlockSpec((tm, d), lambda i: (i, 0))
    return pl.pallas_call(
        _post_kernel,
        grid=(m // tm,),
        in_specs=[row, row, _const_spec((d, d)), _const_spec((1, d)), _const_spec((d, ff)),
                  _const_spec((ff, d))],
        out_specs=row,
        out_shape=jax.ShapeDtypeStruct((m, d), F32),
        compiler_params=_params("parallel"),
        name="post",
    )(x, a, w_mix, g, w_up, w_down)


def _conv_gates(x, g, w_ref):
    d = D_MODEL
    h = _rms(x, g).astype(BF16)
    g_b = _dot(h, w_ref[:, 0:d])
    u = _dot(h, w_ref[:, d:2 * d]) * _dot(h, w_ref[:, 2 * d:3 * d])
    return g_b, u


def _conv_prompt_kernel(x_ref, g_ref, w_ref, cw_ref, a_ref, st_ref, tail_scr, *, seq_blocks):
    i = pl.program_id(0)

    @pl.when(i % seq_blocks == 0)
    def _start_of_sequence():
        tail_scr[...] = jnp.zeros(tail_scr.shape, F32)

    g_b, u = _conv_gates(x_ref[...], g_ref[...], w_ref)
    tm = u.shape[0]
    rows = lax.broadcasted_iota(jnp.int32, u.shape, 0)
    t0 = tail_scr[0:1, :]
    t1 = tail_scr[1:2, :]
    um1 = jnp.where(rows == 0, t1, pltpu.roll(u, 1, axis=0))
    um2 = jnp.where(rows == 0, t0, jnp.where(rows == 1, t1, pltpu.roll(u, 2, axis=0)))
    cw = cw_ref[...]
    conv = cw[0:1, :] * um2 + cw[1:2, :] * um1 + cw[2:3, :] * u
    a_ref[...] = (g_b * conv).astype(BF16)
    last = u[tm - 2:tm, :]
    tail_scr[0:2, :] = last
    st_ref[...] = last


def _conv_prompt(x, g, w_in, conv_w, batch, seq, tm):
    m, d = x.shape
    nb = seq // tm
    row = pl.BlockSpec((tm, d), lambda i: (i, 0))
    kern = functools.partial(_conv_prompt_kernel, seq_blocks=nb)
    return pl.pallas_call(
        kern,
        grid=(m // tm,),
        in_specs=[row, _const_spec((1, d)), _const_spec((d, 3 * d)), _const_spec(conv_w.shape)],
        out_specs=[row, pl.BlockSpec((None, 2, d), lambda i: (i // nb, 0, 0))],
        out_shape=[jax.ShapeDtypeStruct((m, d), BF16), jax.ShapeDtypeStruct((batch, 2, d), F32)],
        scratch_shapes=[pltpu.VMEM((8, d), F32)],
        compiler_params=_params("arbitrary"),
        name="conv_prompt",
    )(x, g, w_in, conv_w)


def _conv_sample_kernel(x_ref, s0_ref, s1_ref, g_ref, w_ref, cw_ref, a_ref, u_ref):
    g_b, u = _conv_gates(x_ref[...], g_ref[...], w_ref)
    cw = cw_ref[...]
    conv = cw[0:1, :] * s0_ref[...] + cw[1:2, :] * s1_ref[...] + cw[2:3, :] * u
    a_ref[...] = (g_b * conv).astype(BF16)
    u_ref[...] = u


def _conv_sample(x, s0, s1, g, w_in, conv_w):
    m, d = x.shape
    full = _const_spec((m, d))
    return pl.pallas_call(
        _conv_sample_kernel,
        grid=(1,),
        in_specs=[full, full, full, _const_spec((1, d)), _const_spec((d, 3 * d)), _const_spec(conv_w.shape)],
        out_specs=[full, full],
        out_shape=[jax.ShapeDtypeStruct((m, d), BF16), jax.ShapeDtypeStruct((m, d), F32)],
        compiler_params=_params("arbitrary"),
        name="conv_sample",
    )(x, s0, s1, g, w_in, conv_w)


def kernel(x_prompt, x_sample, cache_k, cache_v, state_conv, page_table, norm_mix, w_qkv, q_gain, k_gain,
           lambda_q1, lambda_k1, lambda_q2, lambda_k2, subln, w_o, w_in, conv_w, w_out, norm_mlp, w_up, w_down):
    batch, seq, d = x_prompt.shape
    dec_batch, dec_seq, _ = x_sample.shape
    assert d == D_MODEL and dec_seq == 1 and seq % ATTN_BLOCK == 0 and ROW_BLOCK == ATTN_BLOCK
    n_pool = cache_k.shape[1]
    m_p = batch * seq
    m_s = dec_batch * dec_seq
    tm = ROW_BLOCK

    xp = x_prompt.reshape(m_p, d)
    xs = x_sample.reshape(m_s, d)

    lam_init = _lambda_init(0)
    g0 = norm_mix[0].reshape(1, d)
    wqkv = w_qkv[0].astype(BF16)
    reps = d // HEAD_DIM
    qg = (jnp.tile(q_gain[0], reps) * (SCALE * LOG2E)).reshape(1, d)
    kg = jnp.tile(k_gain[0], reps).reshape(1, d)
    idx = jnp.arange(V7X_MXU_DIM) // HEAD_DIM
    gmat = (idx[:, None] == idx[None, :]).astype(BF16)
    lams = [a[0].reshape(1, HEAD_DIM) for a in (lambda_q1, lambda_k1, lambda_q2, lambda_k2)]
    sub = subln[0].reshape(1, V_DIM)

    q_p, kb_p, kt_p, vf_p, vt_p = _qkv_prompt(xp, g0, wqkv, qg, kg, gmat, tm, seq // tm)
    score_bound = HEAD_DIM * SCALE * LOG2E * jnp.max(jnp.abs(q_gain[0])) * jnp.max(jnp.abs(k_gain[0]))

    q_s, kf_s, vf_s = _qkv_sample(xs, g0, wqkv, qg, kg, gmat)
    cache_kt = jnp.transpose(cache_k[0], (0, 2, 3, 4, 1)).reshape(n_pool, d, PAGE_SIZE)
    cache_vr = cache_v[0].reshape(n_pool, PAGE_SIZE * N_HEADS, V_DIM)

    def attend_bounded(q, k, vt, sb, lm, qs, kn, vn, ckt, cvr, pt):
        return tuple(_attn_prompt_with_decode(q, k, vt, sb, lm, qs, kn, vn, ckt, cvr, pt, batch, seq, lam_init))

    def attend_any(q, k, vt, sb, lm, qs, kn, vn, ckt, cvr, pt):
        return (_attn_prompt_online(q, k, vt, sb, lm, batch, seq, lam_init),
                _attn_sample(qs, kn, vn, ckt, cvr, pt, sb, lm, lam_init))

    o_p, o_s = lax.cond(
        score_bound <= MAX_UNSHIFTED_LOG2_SCORE, attend_bounded, attend_any,
        q_p, kb_p, vt_p, sub, lams, q_s.reshape(m_s, 1, d), kf_s.reshape(m_s, 1, d), vf_s.reshape(m_s, 1, d),
        cache_kt, cache_vr, page_table)
    o_s = o_s.reshape(m_s, d).astype(BF16)

    wo = w_o[0].astype(BF16)
    gm0 = norm_mlp[0].reshape(1, d)
    wup0 = w_up[0].astype(BF16)
    wdown0 = w_down[0].astype(BF16)
    xp = _post(xp, o_p, wo, gm0, wup0, wdown0, tm)
    xs = _post(xs, o_s, wo, gm0, wup0, wdown0, m_s)

    g1 = norm_mix[1].reshape(1, d)
    win = w_in[0].astype(BF16)
    a_p, conv_p = _conv_prompt(xp, g1, win, conv_w[0], batch, seq, tm)
    a_s, u_s = _conv_sample(xs, state_conv[0, :, 0, :], state_conv[0, :, 1, :], g1, win, conv_w[0])

    wout = w_out[0].astype(BF16)
    gm1 = norm_mlp[1].reshape(1, d)
    wup1 = w_up[1].astype(BF16)
    wdown1 = w_down[1].astype(BF16)
    xp = _post(xp, a_p, wout, gm1, wup1, wdown1, tm)
    xs = _post(xs, a_s, wout, gm1, wup1, wdown1, m_s)

    conv_s = jnp.stack([state_conv[0, :, 1, :], u_s], axis=1)

    return (xp.reshape(batch, seq, d), xs.reshape(dec_batch, dec_seq, d),
            kt_p.reshape(1, batch, N_HEADS, 2, HEAD_DIM, seq).transpose(0, 1, 5, 2, 3, 4),
            vf_p.reshape(1, batch, seq, N_HEADS, V_DIM),
            kf_s.reshape(1, dec_batch, dec_seq, N_HEADS, 2, HEAD_DIM),
            vf_s.reshape(1, dec_batch, dec_seq, N_HEADS, V_DIM),
            conv_p.reshape(1, batch, 2, d), conv_s.reshape(1, dec_batch, 2, d))
```

```python
import functools
import math

import jax
import jax.numpy as jnp
from jax import lax
from jax.experimental import pallas as pl
from jax.experimental.pallas import tpu as pltpu

F32 = jnp.float32
BF16 = jnp.bfloat16

N_HEADS = 8
HEAD_DIM = 64
V_DIM = 2 * HEAD_DIM
D_MODEL = N_HEADS * V_DIM
EPS = 1e-6
SCALE = HEAD_DIM ** -0.5
NEG = -1e30
LOG2E = math.log2(math.e)
MAX_UNSHIFTED_LOG2_SCORE = 64.0
PAGE_SIZE = 128

V7X_MXU_DIM = 256
V7X_VMEM_BYTES = 64 * 1024 * 1024
V7X_VMEM_LIMIT_BYTES = V7X_VMEM_BYTES - 8 * 1024 * 1024

ROW_BLOCK = 512
ATTN_BLOCK = 512
PAGES_PER_STEP = 8
STAGE_PAGES = 8
DECODE_BUFFERS = 3
FUSED_DECODE_BUFFERS = 4
ATTN_FIXED_WORK = 1.7
FF_CHUNK = 1024
CAST_BLOCK_BYTES = 4 * 1024 * 1024


def _lambda_init(layer):
    return 0.8 - 0.6 * math.exp(-0.3 * layer)


def _rms(x, g):
    ms = jnp.mean(x * x, axis=-1, keepdims=True)
    return x * lax.rsqrt(ms + EPS) * g


def _dot(a, b):
    return jnp.dot(a, b, preferred_element_type=F32)


def _dot_nt(a, b):
    return lax.dot_general(a, b, (((1,), (1,)), ((), ())), preferred_element_type=F32)


def _const_spec(shape):
    nd = len(shape)
    return pl.BlockSpec(shape, lambda *_: (0,) * nd, pipeline_mode=pl.Buffered(1))


def _layer_spec(shape, layer):
    return pl.BlockSpec((None,) + tuple(shape[1:]), lambda *_: (layer, 0, 0), pipeline_mode=pl.Buffered(1))


def _cast_kernel(x_ref, o_ref):
    o_ref[...] = x_ref[...].astype(BF16)


def _to_bf16(w):
    layers, rows, cols = w.shape
    tr = rows
    while tr * cols * 4 > CAST_BLOCK_BYTES and tr % 16 == 0:
        tr //= 2
    spec = pl.BlockSpec((None, tr, cols), lambda l, i: (l, i, 0))
    return pl.pallas_call(
        _cast_kernel,
        grid=(layers, rows // tr),
        in_specs=[spec],
        out_specs=spec,
        out_shape=jax.ShapeDtypeStruct(w.shape, BF16),
        compiler_params=_params("parallel", "parallel"),
        name="cast_bf16",
    )(w)


def _params(*sem):
    return pltpu.CompilerParams(dimension_semantics=sem, vmem_limit_bytes=V7X_VMEM_LIMIT_BYTES)


def _group_mean_square(t, gmat):
    sq = t * t
    hi = sq.astype(BF16)
    lo = (sq - hi.astype(F32)).astype(BF16)
    parts = []
    for c in range(t.shape[1] // V7X_MXU_DIM):
        sl = slice(c * V7X_MXU_DIM, (c + 1) * V7X_MXU_DIM)
        parts.append(_dot(hi[:, sl], gmat) + _dot(lo[:, sl], gmat))
    return jnp.concatenate(parts, axis=1) * (1.0 / HEAD_DIM)


def _qkv_values(x_ref, g_ref, w_ref, qg_ref, kg_ref, gmat_ref):
    d = D_MODEL
    h = _rms(x_ref[...], g_ref[...]).astype(BF16)
    gmat = gmat_ref[...]
    tq = _dot(h, w_ref[:, 0:d])
    qn = tq * lax.rsqrt(_group_mean_square(tq, gmat) + EPS) * qg_ref[...]
    tk = _dot(h, w_ref[:, d:2 * d])
    kn = tk * lax.rsqrt(_group_mean_square(tk, gmat) + EPS) * kg_ref[...]
    tv = _dot(h, w_ref[:, 2 * d:3 * d])
    return qn, kn, tv


def _qkv_prompt_kernel(x_ref, g_ref, w_ref, qg_ref, kg_ref, gmat_ref, q_ref, kb_ref, kt_ref, vf_ref, vt_ref):
    qn, kn, tv = _qkv_values(x_ref, g_ref, w_ref, qg_ref, kg_ref, gmat_ref)
    q_ref[...] = qn.astype(BF16)
    kb_ref[...] = kn.astype(BF16)
    kt_ref[...] = kn.T
    vf_ref[...] = tv
    vt_ref[...] = tv.T.astype(BF16)


def _qkv_sample_kernel(x_ref, g_ref, w_ref, qg_ref, kg_ref, gmat_ref, q_ref, kf_ref, vf_ref):
    qn, kn, tv = _qkv_values(x_ref, g_ref, w_ref, qg_ref, kg_ref, gmat_ref)
    q_ref[...] = qn.astype(BF16)
    kf_ref[...] = kn
    vf_ref[...] = tv


def _qkv_in_specs(tm, d):
    return [pl.BlockSpec((tm, d), lambda i: (i, 0)), _const_spec((1, d)), _layer_spec((1, d, 3 * d), 0),
            _const_spec((1, d)), _const_spec((1, d)), _const_spec((V7X_MXU_DIM, V7X_MXU_DIM))]


def _qkv_prompt(x, g, w, qg, kg, gmat, tm, nb):
    m, d = x.shape
    batch = m // (tm * nb)
    row = pl.BlockSpec((tm, d), lambda i: (i, 0))
    return pl.pallas_call(
        _qkv_prompt_kernel,
        grid=(m // tm,),
        in_specs=_qkv_in_specs(tm, d),
        out_specs=[row, row, pl.BlockSpec((None, d, tm), lambda i: (i // nb, 0, i % nb)), row,
                   pl.BlockSpec((None, None, d, tm), lambda i: (i // nb, i % nb, 0, 0))],
        out_shape=[jax.ShapeDtypeStruct((m, d), BF16), jax.ShapeDtypeStruct((m, d), BF16),
                   jax.ShapeDtypeStruct((batch, d, tm * nb), F32), jax.ShapeDtypeStruct((m, d), F32),
                   jax.ShapeDtypeStruct((batch, nb, d, tm), BF16)],
        compiler_params=_params("parallel"),
        name="qkv_prompt",
    )(x, g, w, qg, kg, gmat)


def _qkv_sample(x, g, w, qg, kg, gmat):
    m, d = x.shape
    row = pl.BlockSpec((m, d), lambda i: (i, 0))
    return pl.pallas_call(
        _qkv_sample_kernel,
        grid=(1,),
        in_specs=_qkv_in_specs(m, d),
        out_specs=[row, row, row],
        out_shape=[jax.ShapeDtypeStruct((m, d), BF16), jax.ShapeDtypeStruct((m, d), F32),
                   jax.ShapeDtypeStruct((m, d), F32)],
        compiler_params=_params("parallel"),
        name="qkv_sample",
    )(x, g, w, qg, kg, gmat)


def _diff_lambda(lq1, lk1, lq2, lk2, lam_init):
    d1 = jnp.sum(lq1 * lk1, axis=-1, keepdims=True)
    d2 = jnp.sum(lq2 * lk2, axis=-1, keepdims=True)
    return jnp.exp(d1) - jnp.exp(d2) + lam_init


def _decode_scratch(nbuf, pps):
    d = D_MODEL
    nhc = 2 * N_HEADS
    return [pltpu.VMEM((nbuf, pps, d, PAGE_SIZE), F32),
            pltpu.VMEM((nbuf, pps, PAGE_SIZE * N_HEADS, V_DIM), F32),
            pltpu.SemaphoreType.DMA((nbuf,)),
            pltpu.VMEM((d, min(pps, STAGE_PAGES) * PAGE_SIZE), BF16),
            pltpu.VMEM((N_HEADS // 2, min(pps, STAGE_PAGES) * PAGE_SIZE, 2 * V_DIM), BF16),
            pltpu.VMEM((nhc, d), BF16), pltpu.VMEM((nhc, 1), F32),
            pltpu.VMEM((nhc, 1), F32), pltpu.VMEM((N_HEADS // 2, nhc, 2 * V_DIM), F32)]


def _decode_program(pt_ref, q_ref, kn_ref, vn_ref, sub_ref, lq1_ref, lk1_ref, lq2_ref, lk2_ref,
                    kt_hbm, v_hbm, o_ref, kbuf, vbuf, sem, kb_scr, vb_scr, qbd_scr, m_scr, l_scr, acc_scr,
                    *, pps, steps, nbuf, lam_init):
    nhc = 2 * N_HEADS
    total = q_ref.shape[0] * steps

    def chunk_copies(t, slot):
        b = t // steps
        c = t % steps
        copies = []
        for j in range(pps):
            page = pt_ref[b, c * pps + j]
            copies.append(pltpu.make_async_copy(kt_hbm.at[page], kbuf.at[slot, j], sem.at[slot]))
            copies.append(pltpu.make_async_copy(v_hbm.at[page], vbuf.at[slot, j], sem.at[slot]))
        return copies

    def init(b):
        row = lax.broadcasted_iota(jnp.int32, (nhc, D_MODEL), 0)
        col = lax.broadcasted_iota(jnp.int32, (nhc, D_MODEL), 1)
        q = jnp.broadcast_to(q_ref[b].astype(F32), (nhc, D_MODEL))
        qbd = jnp.where(col // HEAD_DIM == row, q, 0.0)
        qbd_scr[...] = qbd.astype(BF16)
        m_scr[...] = jnp.sum(qbd * kn_ref[b], axis=1, keepdims=True)
        l_scr[...] = jnp.ones(l_scr.shape, F32)
        vn = jnp.broadcast_to(vn_ref[b], (nhc, D_MODEL))
        for hp in range(N_HEADS // 2):
            acc_scr[hp] = vn[:, hp * 2 * V_DIM:(hp + 1) * 2 * V_DIM]

    def accumulate(slot):
        groups = [range(g, min(g + STAGE_PAGES, pps)) for g in range(0, pps, STAGE_PAGES)]
        qbd = qbd_scr[...]
        parts = []
        for pages in groups:
            for i, j in enumerate(pages):
                kb_scr[:, i * PAGE_SIZE:(i + 1) * PAGE_SIZE] = kbuf[slot, j].astype(BF16)
            parts.append(_dot(qbd, kb_scr[:, :len(pages) * PAGE_SIZE]))
        s = jnp.concatenate(parts, axis=1)
        m_old = m_scr[...]
        m_new = jnp.maximum(m_old, jnp.max(s, axis=1, keepdims=True))
        alpha = jnp.exp2(m_old - m_new)
        p = jnp.exp2(s - m_new)
        l_scr[...] = alpha * l_scr[...] + jnp.sum(p, axis=1, keepdims=True)
        pb = p.astype(BF16)
        m_scr[...] = m_new
        for hp in range(N_HEADS // 2):
            acc_scr[hp] = alpha * acc_scr[hp]
        for pages in groups:
            n_tok = len(pages) * PAGE_SIZE
            for i, j in enumerate(pages):
                tok = slice(i * PAGE_SIZE, (i + 1) * PAGE_SIZE)
                for h in range(N_HEADS):
                    lanes = slice((h % 2) * V_DIM, (h % 2 + 1) * V_DIM)
                    vb_scr[h // 2, tok, lanes] = vbuf[slot, j, pl.ds(h, PAGE_SIZE, stride=N_HEADS), :].astype(BF16)
            pg = pb[:, pages[0] * PAGE_SIZE:pages[0] * PAGE_SIZE + n_tok]
            for hp in range(N_HEADS // 2):
                acc_scr[hp] += _dot(pg, vb_scr[hp, :n_tok, :])

    def finish(b):
        lam = _diff_lambda(lq1_ref[...], lk1_ref[...], lq2_ref[...], lk2_ref[...], lam_init)
        row = lax.broadcasted_iota(jnp.int32, (nhc, V_DIM), 0)
        inv_l = 1.0 / l_scr[...]
        outs = []
        for h in range(N_HEADS):
            a = acc_scr[h // 2][:, (h % 2) * V_DIM:(h % 2 + 1) * V_DIM] * inv_l
            a = jnp.where(row == 2 * h, a, 0.0) - lam * jnp.where(row == 2 * h + 1, a, 0.0)
            outs.append(jnp.sum(a, axis=0, keepdims=True))
        o = jnp.concatenate(outs, axis=0)
        ms = jnp.mean(o * o, axis=1, keepdims=True)
        o_ref[b] = ((o * lax.rsqrt(ms + EPS)) * sub_ref[...]) * (1.0 - lam_init)

    def prologue():
        for t0 in range(nbuf - 1):
            for cp in chunk_copies(t0, t0):
                cp.start()

    def process(t):
        slot = lax.rem(t, nbuf)
        ahead = t + (nbuf - 1)

        @pl.when(ahead < total)
        def _prefetch():
            for cp in chunk_copies(ahead, lax.rem(ahead, nbuf)):
                cp.start()

        for cp in chunk_copies(t, slot):
            cp.wait()
        b = t // steps
        c = t % steps
        pl.when(c == 0)(lambda: init(b))
        accumulate(slot)
        pl.when(c == steps - 1)(lambda: finish(b))

    return prologue, process


def _decode_kernel(*refs, **cfg):
    prologue, process = _decode_program(*refs, **cfg)
    t = pl.program_id(0)
    pl.when(t == 0)(prologue)
    process(t)


def _attn_sample(q, k_new, v_new, cache_kt, cache_v, page_table, sub, lams, lam_init):
    b = q.shape[0]
    n_pages = page_table.shape[1]
    pps = PAGES_PER_STEP
    nbuf = DECODE_BUFFERS
    assert n_pages % pps == 0
    steps = n_pages // pps

    def whole(shape):
        nd = len(shape)
        return pl.BlockSpec(shape, lambda i, pt: (0,) * nd)

    kern = functools.partial(_decode_kernel, pps=pps, steps=steps, nbuf=nbuf, lam_init=lam_init)
    grid_spec = pltpu.PrefetchScalarGridSpec(
        num_scalar_prefetch=1,
        grid=(b * steps,),
        in_specs=[whole(q.shape), whole(k_new.shape), whole(v_new.shape), whole((1, V_DIM))]
        + [whole((1, HEAD_DIM))] * 4
        + [pl.BlockSpec(memory_space=pl.ANY), pl.BlockSpec(memory_space=pl.ANY)],
        out_specs=whole((b, N_HEADS, V_DIM)),
        scratch_shapes=_decode_scratch(nbuf, pps),
    )
    return pl.pallas_call(
        kern,
        grid_spec=grid_spec,
        out_shape=jax.ShapeDtypeStruct((b, N_HEADS, V_DIM), F32),
        compiler_params=_params("arbitrary"),
        name="attn_sample",
    )(page_table, q, k_new, v_new, sub, *lams, cache_kt, cache_v)


def _split_components(q):
    lane = lax.broadcasted_iota(jnp.int32, q.shape, 1)
    zero = jnp.zeros_like(q)
    return jnp.where(lane < HEAD_DIM, q, zero), jnp.where(lane >= HEAD_DIM, q, zero)


def _causal_keep(shape):
    kpos = lax.broadcasted_iota(jnp.int32, shape, 0)
    qpos = lax.broadcasted_iota(jnp.int32, shape, 1)
    return kpos <= qpos


def _attn_finish(acc0, l0, acc1, l1, sub_ref, lam_refs, o_ref, lam_init):
    lam = _diff_lambda(*(r[...] for r in lam_refs), lam_init)
    o_t = acc0 / l0 - lam * (acc1 / l1)
    ms = jnp.mean(o_t * o_t, axis=0, keepdims=True)
    o_n = (o_t * lax.rsqrt(ms + EPS)).T
    o_ref[...] = ((o_n * sub_ref[...]) * (1.0 - lam_init)).astype(BF16)


def _chunks_per_query_block(nb, chunks):
    weights = [ATTN_FIXED_WORK + i for i in range(nb)]
    ideal = [chunks * w / sum(weights) for w in weights]
    counts = [int(x) for x in ideal]
    by_remainder = sorted(range(nb), key=lambda i: ideal[i] - counts[i], reverse=True)
    for i in by_remainder[:chunks - sum(counts)]:
        counts[i] += 1
    return counts


def _attn_bounded_kernel(pt_ref, q_ref, k_ref, vt_ref, sub_ref, lq1_ref, lk1_ref, lq2_ref, lk2_ref,
                         qs_ref, kn_ref, vn_ref, kt_hbm, v_hbm, o_ref, os_ref,
                         s_even, s_odd, l_scr, acc_scr, *decode_scratch,
                         blk, lam_init, chunks_per_group, decode_cfg):
    half = blk // 2
    qi = pl.program_id(2)
    group = pl.program_id(0) * pl.num_programs(1) + pl.program_id(1)

    prologue, process = _decode_program(
        pt_ref, qs_ref, kn_ref, vn_ref, sub_ref, lq1_ref, lk1_ref, lq2_ref, lk2_ref, kt_hbm, v_hbm, os_ref,
        *decode_scratch, lam_init=lam_init, **decode_cfg)
    counts = _chunks_per_query_block(k_ref.shape[0] // blk, chunks_per_group)
    first = group * chunks_per_group + sum(jnp.where(qi > i, n, 0) for i, n in enumerate(counts))
    count = sum(jnp.where(qi == i, n, 0) for i, n in enumerate(counts))
    pl.when(jnp.logical_and(group == 0, qi == 0))(prologue)

    def decode_body(i, carry):
        process(first + i)
        return carry

    lax.fori_loop(0, count, decode_body, 0)

    qc = _split_components(q_ref[...])
    l_scr[...] = jnp.zeros(l_scr.shape, F32)
    acc_scr[...] = jnp.zeros(acc_scr.shape, F32)

    def scores(j, hf, dst):
        kb = k_ref[pl.ds(pl.multiple_of(j * blk + hf * half, half), half), :]
        for c in range(2):
            dst[c] = _dot_nt(kb, qc[c])

    def consume(j, hf, src, masked):
        vb = vt_ref[j, :, hf * half:(hf + 1) * half]
        for c in range(2):
            p = jnp.exp2(src[c])
            if masked:
                kpos = lax.broadcasted_iota(jnp.int32, p.shape, 0) + hf * half
                qpos = lax.broadcasted_iota(jnp.int32, p.shape, 1)
                p = jnp.where(kpos <= qpos, p, 0.0)
            l_scr[c] += jnp.sum(p.reshape(half // 8, 8, blk), axis=0)
            acc_scr[c] += _dot(vb, p.astype(BF16))

    scores(0, 0, s_even)

    def body(j, carry):
        scores(j, 1, s_odd)
        consume(j, 0, s_even, masked=False)
        scores(j + 1, 0, s_even)
        consume(j, 1, s_odd, masked=False)
        return carry

    lax.fori_loop(0, qi, body, 0)
    kb_hi = k_ref[pl.ds(pl.multiple_of(qi * blk + half, half), half), :]
    for c in range(2):
        s_odd[c, :, half:] = _dot_nt(kb_hi, qc[c][half:, :])
    consume(qi, 0, s_even, masked=True)
    vb_hi = vt_ref[qi, :, half:]
    for c in range(2):
        p = jnp.exp2(s_odd[c, :, half:])
        p = jnp.where(_causal_keep(p.shape), p, 0.0)
        l_scr[c, :, half:] += jnp.sum(p.reshape(half // 8, 8, half), axis=0)
        acc_scr[c, :, half:] += _dot(vb_hi, p.astype(BF16))
    l0 = jnp.sum(l_scr[0], axis=0, keepdims=True)
    l1 = jnp.sum(l_scr[1], axis=0, keepdims=True)
    _attn_finish(acc_scr[0], l0, acc_scr[1], l1, sub_ref, (lq1_ref, lk1_ref, lq2_ref, lk2_ref), o_ref, lam_init)


def _attn_online_kernel(q_ref, k_ref, vt_ref, sub_ref, lq1_ref, lk1_ref, lq2_ref, lk2_ref, o_ref,
                        m_scr, l_scr, acc_scr, *, blk, lam_init):
    qi = pl.program_id(2)
    qc = _split_components(q_ref[...])
    m_scr[...] = jnp.full(m_scr.shape, NEG, F32)
    l_scr[...] = jnp.zeros(l_scr.shape, F32)
    acc_scr[...] = jnp.zeros(acc_scr.shape, F32)

    def update(j, masked):
        kb = k_ref[pl.ds(pl.multiple_of(j * blk, blk), blk), :]
        vb = vt_ref[j]
        for c in range(2):
            s = _dot_nt(kb, qc[c])
            if masked:
                s = jnp.where(_causal_keep(s.shape), s, NEG)
            m_old = m_scr[c]
            m_new = jnp.maximum(m_old, jnp.max(s, axis=0, keepdims=True))
            alpha = jnp.exp2(m_old - m_new)
            p = jnp.exp2(s - m_new)
            l_scr[c] = alpha * l_scr[c] + jnp.sum(p, axis=0, keepdims=True)
            acc_scr[c] = alpha * acc_scr[c] + _dot(vb, p.astype(BF16))
            m_scr[c] = m_new

    def body(j, carry):
        update(j, masked=False)
        return carry

    lax.fori_loop(0, qi, body, 0)
    update(qi, masked=True)
    _attn_finish(acc_scr[0], l_scr[0], acc_scr[1], l_scr[1], sub_ref, (lq1_ref, lk1_ref, lq2_ref, lk2_ref),
                 o_ref, lam_init)


def _attn_prompt_online(q, k, vt, sub, lams, batch, seq, lam_init):
    m, d = q.shape
    blk = ATTN_BLOCK
    nb = seq // blk
    small = _const_spec((1, HEAD_DIM))
    return pl.pallas_call(
        functools.partial(_attn_online_kernel, blk=blk, lam_init=lam_init),
        grid=(batch, N_HEADS, nb),
        in_specs=[pl.BlockSpec((blk, V_DIM), lambda b, h, i: (b * nb + i, h)),
                  pl.BlockSpec((seq, V_DIM), lambda b, h, i: (b, h)),
                  pl.BlockSpec((None, nb, V_DIM, blk), lambda b, h, i: (b, 0, h, 0)),
                  _const_spec((1, V_DIM)), small, small, small, small],
        out_specs=pl.BlockSpec((blk, V_DIM), lambda b, h, i: (b * nb + i, h)),
        out_shape=jax.ShapeDtypeStruct((m, d), BF16),
        scratch_shapes=[pltpu.VMEM((2, 1, blk), F32), pltpu.VMEM((2, 1, blk), F32),
                        pltpu.VMEM((2, V_DIM, blk), F32)],
        compiler_params=_params("parallel", "parallel", "arbitrary"),
        name="attn_prompt_online",
    )(q, k, vt, sub, *lams)


def _attn_prompt_with_decode(q, k, vt, sub, lams, q_s, k_new, v_new, cache_kt, cache_v, page_table,
                             batch, seq, lam_init):
    m, d = q.shape
    blk = ATTN_BLOCK
    nb = seq // blk
    dec_b = q_s.shape[0]
    n_pages = page_table.shape[1]
    pps = PAGES_PER_STEP
    nbuf = FUSED_DECODE_BUFFERS
    groups = batch * N_HEADS
    assert n_pages % pps == 0 and (dec_b * (n_pages // pps)) % groups == 0
    steps = n_pages // pps

    def whole(shape):
        nd = len(shape)
        return pl.BlockSpec(shape, lambda b, h, i, pt: (0,) * nd)

    kern = functools.partial(
        _attn_bounded_kernel, blk=blk, lam_init=lam_init, chunks_per_group=dec_b * steps // groups,
        decode_cfg=dict(pps=pps, steps=steps, nbuf=nbuf))
    stage = pltpu.VMEM((2, blk // 2, blk), F32)
    grid_spec = pltpu.PrefetchScalarGridSpec(
        num_scalar_prefetch=1,
        grid=(batch, N_HEADS, nb),
        in_specs=[pl.BlockSpec((blk, V_DIM), lambda b, h, i, pt: (b * nb + i, h)),
                  pl.BlockSpec((seq, V_DIM), lambda b, h, i, pt: (b, h)),
                  pl.BlockSpec((None, nb, V_DIM, blk), lambda b, h, i, pt: (b, 0, h, 0)),
                  whole((1, V_DIM))] + [whole((1, HEAD_DIM))] * 4
        + [whole(q_s.shape), whole(k_new.shape), whole(v_new.shape),
           pl.BlockSpec(memory_space=pl.ANY), pl.BlockSpec(memory_space=pl.ANY)],
        out_specs=[pl.BlockSpec((blk, V_DIM), lambda b, h, i, pt: (b * nb + i, h)),
                   whole((dec_b, N_HEADS, V_DIM))],
        scratch_shapes=[stage, stage, pltpu.VMEM((2, 8, blk), F32), pltpu.VMEM((2, V_DIM, blk), F32)]
        + _decode_scratch(nbuf, pps),
    )
    return pl.pallas_call(
        kern,
        grid_spec=grid_spec,
        out_shape=[jax.ShapeDtypeStruct((m, d), BF16), jax.ShapeDtypeStruct((dec_b, N_HEADS, V_DIM), F32)],
        compiler_params=_params("arbitrary", "arbitrary", "arbitrary"),
        name="attn_prompt_decode",
    )(page_table, q, k, vt, sub, *lams, q_s, k_new, v_new, cache_kt, cache_v)


def _post_kernel(x_ref, a_ref, wmix_ref, g_ref, wup_ref, wdown_ref, o_ref):
    x1 = x_ref[...] + _dot(a_ref[...], wmix_ref[...])
    h = _rms(x1, g_ref[...]).astype(BF16)
    acc = x1
    ff = wup_ref.shape[1]
    for c in range(ff // FF_CHUNK):
        sl = slice(c * FF_CHUNK, (c + 1) * FF_CHUNK)
        u = jnp.maximum(_dot(h, wup_ref[:, sl]), 0.0)
        acc = acc + _dot((u * u).astype(BF16), wdown_ref[sl, :])
    o_ref[...] = acc


def _post(x, a, w_mix, g, w_up, w_down, layer, tm):
    m, d = x.shape
    row = pl.BlockSpec((tm, d), lambda i: (i, 0))
    return pl.pallas_call(
        _post_kernel,
        grid=(m // tm,),
        in_specs=[row, row, _layer_spec(w_mix.shape, 0), _const_spec((1, d)), _layer_spec(w_up.shape, layer),
                  _layer_spec(w_down.shape, layer)],
        out_specs=row,
        out_shape=jax.ShapeDtypeStruct((m, d), F32),
        compiler_params=_params("parallel"),
        name="post",
    )(x, a, w_mix, g, w_up, w_down)


def _conv_gates(x, g, w_ref):
    d = D_MODEL
    h = _rms(x, g).astype(BF16)
    g_b = _dot(h, w_ref[:, 0:d])
    u = _dot(h, w_ref[:, d:2 * d]) * _dot(h, w_ref[:, 2 * d:3 * d])
    return g_b, u


def _conv_prompt_kernel(x_ref, g_ref, w_ref, cw_ref, a_ref, st_ref, tail_scr, *, seq_blocks):
    i = pl.program_id(0)

    @pl.when(i % seq_blocks == 0)
    def _start_of_sequence():
        tail_scr[...] = jnp.zeros(tail_scr.shape, F32)

    g_b, u = _conv_gates(x_ref[...], g_ref[...], w_ref)
    tm = u.shape[0]
    rows = lax.broadcasted_iota(jnp.int32, u.shape, 0)
    t0 = tail_scr[0:1, :]
    t1 = tail_scr[1:2, :]
    um1 = jnp.where(rows == 0, t1, pltpu.roll(u, 1, axis=0))
    um2 = jnp.where(rows == 0, t0, jnp.where(rows == 1, t1, pltpu.roll(u, 2, axis=0)))
    cw = cw_ref[...]
    conv = cw[0:1, :] * um2 + cw[1:2, :] * um1 + cw[2:3, :] * u
    a_ref[...] = (g_b * conv).astype(BF16)
    last = u[tm - 2:tm, :]
    tail_scr[0:2, :] = last
    st_ref[...] = last


def _conv_prompt(x, g, w_in, conv_w, batch, seq, tm):
    m, d = x.shape
    nb = seq // tm
    row = pl.BlockSpec((tm, d), lambda i: (i, 0))
    kern = functools.partial(_conv_prompt_kernel, seq_blocks=nb)
    return pl.pallas_call(
        kern,
        grid=(m // tm,),
        in_specs=[row, _const_spec((1, d)), _layer_spec(w_in.shape, 0), _const_spec(conv_w.shape)],
        out_specs=[row, pl.BlockSpec((None, 2, d), lambda i: (i // nb, 0, 0))],
        out_shape=[jax.ShapeDtypeStruct((m, d), BF16), jax.ShapeDtypeStruct((batch, 2, d), F32)],
        scratch_shapes=[pltpu.VMEM((8, d), F32)],
        compiler_params=_params("arbitrary"),
        name="conv_prompt",
    )(x, g, w_in, conv_w)


def _conv_sample_kernel(x_ref, s0_ref, s1_ref, g_ref, w_ref, cw_ref, a_ref, u_ref):
    g_b, u = _conv_gates(x_ref[...], g_ref[...], w_ref)
    cw = cw_ref[...]
    conv = cw[0:1, :] * s0_ref[...] + cw[1:2, :] * s1_ref[...] + cw[2:3, :] * u
    a_ref[...] = (g_b * conv).astype(BF16)
    u_ref[...] = u


def _conv_sample(x, s0, s1, g, w_in, conv_w):
    m, d = x.shape
    full = _const_spec((m, d))
    return pl.pallas_call(
        _conv_sample_kernel,
        grid=(1,),
        in_specs=[full, full, full, _const_spec((1, d)), _layer_spec(w_in.shape, 0), _const_spec(conv_w.shape)],
        out_specs=[full, full],
        out_shape=[jax.ShapeDtypeStruct((m, d), BF16), jax.ShapeDtypeStruct((m, d), F32)],
        compiler_params=_params("arbitrary"),
        name="conv_sample",
    )(x, s0, s1, g, w_in, conv_w)


def kernel(x_prompt, x_sample, cache_k, cache_v, state_conv, page_table, norm_mix, w_qkv, q_gain, k_gain,
           lambda_q1, lambda_k1, lambda_q2, lambda_k2, subln, w_o, w_in, conv_w, w_out, norm_mlp, w_up, w_down):
    batch, seq, d = x_prompt.shape
    dec_batch, dec_seq, _ = x_sample.shape
    assert d == D_MODEL and dec_seq == 1 and seq % ATTN_BLOCK == 0 and ROW_BLOCK == ATTN_BLOCK
    n_pool = cache_k.shape[1]
    m_p = batch * seq
    m_s = dec_batch * dec_seq
    tm = ROW_BLOCK

    xp = x_prompt.reshape(m_p, d)
    xs = x_sample.reshape(m_s, d)

    lam_init = _lambda_init(0)
    g0 = norm_mix[0].reshape(1, d)
    wqkv = _to_bf16(w_qkv)
    reps = d // HEAD_DIM
    qg = (jnp.tile(q_gain[0], reps) * (SCALE * LOG2E)).reshape(1, d)
    kg = jnp.tile(k_gain[0], reps).reshape(1, d)
    idx = jnp.arange(V7X_MXU_DIM) // HEAD_DIM
    gmat = (idx[:, None] == idx[None, :]).astype(BF16)
    lams = [a[0].reshape(1, HEAD_DIM) for a in (lambda_q1, lambda_k1, lambda_q2, lambda_k2)]
    sub = subln[0].reshape(1, V_DIM)

    q_p, kb_p, kt_p, vf_p, vt_p = _qkv_prompt(xp, g0, wqkv, qg, kg, gmat, tm, seq // tm)
    score_bound = HEAD_DIM * SCALE * LOG2E * jnp.max(jnp.abs(q_gain[0])) * jnp.max(jnp.abs(k_gain[0]))

    q_s, kf_s, vf_s = _qkv_sample(xs, g0, wqkv, qg, kg, gmat)
    cache_kt = jnp.transpose(cache_k[0], (0, 2, 3, 4, 1)).reshape(n_pool, d, PAGE_SIZE)
    cache_vr = cache_v[0].reshape(n_pool, PAGE_SIZE * N_HEADS, V_DIM)

    def attend_bounded(q, k, vt, sb, lm, qs, kn, vn, ckt, cvr, pt):
        return tuple(_attn_prompt_with_decode(q, k, vt, sb, lm, qs, kn, vn, ckt, cvr, pt, batch, seq, lam_init))

    def attend_any(q, k, vt, sb, lm, qs, kn, vn, ckt, cvr, pt):
        return (_attn_prompt_online(q, k, vt, sb, lm, batch, seq, lam_init),
                _attn_sample(qs, kn, vn, ckt, cvr, pt, sb, lm, lam_init))

    o_p, o_s = lax.cond(
        score_bound <= MAX_UNSHIFTED_LOG2_SCORE, attend_bounded, attend_any,
        q_p, kb_p, vt_p, sub, lams, q_s.reshape(m_s, 1, d), kf_s.reshape(m_s, 1, d), vf_s.reshape(m_s, 1, d),
        cache_kt, cache_vr, page_table)
    o_s = o_s.reshape(m_s, d).astype(BF16)

    wo = _to_bf16(w_o)
    wup = _to_bf16(w_up)
    wdown = _to_bf16(w_down)
    gm0 = norm_mlp[0].reshape(1, d)
    xp = _post(xp, o_p, wo, gm0, wup, wdown, 0, tm)
    xs = _post(xs, o_s, wo, gm0, wup, wdown, 0, m_s)

    g1 = norm_mix[1].reshape(1, d)
    win = _to_bf16(w_in)
    a_p, conv_p = _conv_prompt(xp, g1, win, conv_w[0], batch, seq, tm)
    a_s, u_s = _conv_sample(xs, state_conv[0, :, 0, :], state_conv[0, :, 1, :], g1, win, conv_w[0])

    wout = _to_bf16(w_out)
    gm1 = norm_mlp[1].reshape(1, d)
    xp = _post(xp, a_p, wout, gm1, wup, wdown, 1, tm)
    xs = _post(xs, a_s, wout, gm1, wup, wdown, 1, m_s)

    conv_s = jnp.stack([state_conv[0, :, 1, :], u_s], axis=1)

    return (xp.reshape(batch, seq, d), xs.reshape(dec_batch, dec_seq, d),
            kt_p.reshape(1, batch, N_HEADS, 2, HEAD_DIM, seq).transpose(0, 1, 5, 2, 3, 4),
            vf_p.reshape(1, batch, seq, N_HEADS, V_DIM),
            kf_s.reshape(1, dec_batch, dec_seq, N_HEADS, 2, HEAD_DIM),
            vf_s.reshape(1, dec_batch, dec_seq, N_HEADS, V_DIM),
            conv_p.reshape(1, batch, 2, d), conv_s.reshape(1, dec_batch, 2, d))
```

```python
import functools
import math

import jax
import jax.numpy as jnp
from jax import lax
from jax.experimental import pallas as pl
from jax.experimental.pallas import tpu as pltpu

F32 = jnp.float32
BF16 = jnp.bfloat16

N_HEADS = 8
HEAD_DIM = 64
V_DIM = 2 * HEAD_DIM
D_MODEL = N_HEADS * V_DIM
EPS = 1e-6
SCALE = HEAD_DIM ** -0.5
NEG = -1e30
LOG2E = math.log2(math.e)
MAX_UNSHIFTED_LOG2_SCORE = 64.0
PAGE_SIZE = 128

V7X_MXU_DIM = 256
V7X_VMEM_BYTES = 64 * 1024 * 1024
V7X_VMEM_LIMIT_BYTES = V7X_VMEM_BYTES - 8 * 1024 * 1024

ROW_BLOCK = 512
ATTN_BLOCK = 512
PAGES_PER_STEP = 8
STAGE_PAGES = 8
DECODE_BUFFERS = 3
FUSED_DECODE_BUFFERS = 4
ATTN_FIXED_WORK = 1.7
FF_CHUNK = 1024
CAST_BLOCK_BYTES = 8 * 1024 * 1024
CONV_COLS = 256


def _lambda_init(layer):
    return 0.8 - 0.6 * math.exp(-0.3 * layer)


def _rms(x, g):
    ms = jnp.mean(x * x, axis=-1, keepdims=True)
    return x * lax.rsqrt(ms + EPS) * g


def _dot(a, b):
    return jnp.dot(a, b, preferred_element_type=F32)


def _dot_nt(a, b):
    return lax.dot_general(a, b, (((1,), (1,)), ((), ())), preferred_element_type=F32)


def _const_spec(shape):
    nd = len(shape)
    return pl.BlockSpec(shape, lambda *_: (0,) * nd, pipeline_mode=pl.Buffered(1))


def _layer_spec(shape, layer):
    return pl.BlockSpec((None,) + tuple(shape[1:]), lambda *_: (layer, 0, 0), pipeline_mode=pl.Buffered(1))


def _cast_kernel(x_ref, o_ref):
    o_ref[...] = x_ref[...].astype(BF16)


def _to_bf16(w):
    layers, rows, cols = w.shape
    tr = rows
    while tr * cols * 4 > CAST_BLOCK_BYTES and tr % 16 == 0:
        tr //= 2
    spec = pl.BlockSpec((None, tr, cols), lambda l, i: (l, i, 0))
    return pl.pallas_call(
        _cast_kernel,
        grid=(layers, rows // tr),
        in_specs=[spec],
        out_specs=spec,
        out_shape=jax.ShapeDtypeStruct(w.shape, BF16),
        compiler_params=_params("parallel", "parallel"),
        name="cast_bf16",
    )(w)


def _params(*sem):
    return pltpu.CompilerParams(dimension_semantics=sem, vmem_limit_bytes=V7X_VMEM_LIMIT_BYTES)


def _group_mean_square(t, gmat):
    sq = t * t
    hi = sq.astype(BF16)
    lo = (sq - hi.astype(F32)).astype(BF16)
    parts = []
    for c in range(t.shape[1] // V7X_MXU_DIM):
        sl = slice(c * V7X_MXU_DIM, (c + 1) * V7X_MXU_DIM)
        parts.append(_dot(hi[:, sl], gmat) + _dot(lo[:, sl], gmat))
    return jnp.concatenate(parts, axis=1) * (1.0 / HEAD_DIM)


def _qkv_values(x_ref, g_ref, w_ref, qg_ref, kg_ref, gmat_ref):
    d = D_MODEL
    h = _rms(x_ref[...], g_ref[...]).astype(BF16)
    gmat = gmat_ref[...]
    tq = _dot(h, w_ref[:, 0:d])
    qn = tq * lax.rsqrt(_group_mean_square(tq, gmat) + EPS) * qg_ref[...]
    tk = _dot(h, w_ref[:, d:2 * d])
    kn = tk * lax.rsqrt(_group_mean_square(tk, gmat) + EPS) * kg_ref[...]
    tv = _dot(h, w_ref[:, 2 * d:3 * d])
    return qn, kn, tv


def _qkv_prompt_kernel(x_ref, g_ref, w_ref, qg_ref, kg_ref, gmat_ref, q_ref, kb_ref, kt_ref, vf_ref, vt_ref):
    qn, kn, tv = _qkv_values(x_ref, g_ref, w_ref, qg_ref, kg_ref, gmat_ref)
    q_ref[...] = qn.astype(BF16)
    kb_ref[...] = kn.astype(BF16)
    kt_ref[...] = kn.T
    vf_ref[...] = tv
    vt_ref[...] = tv.T.astype(BF16)


def _qkv_sample_kernel(x_ref, g_ref, w_ref, qg_ref, kg_ref, gmat_ref, q_ref, kf_ref, vf_ref):
    qn, kn, tv = _qkv_values(x_ref, g_ref, w_ref, qg_ref, kg_ref, gmat_ref)
    q_ref[...] = qn.astype(BF16)
    kf_ref[...] = kn
    vf_ref[...] = tv


def _qkv_in_specs(tm, d):
    return [pl.BlockSpec((tm, d), lambda i: (i, 0)), _const_spec((1, d)), _layer_spec((1, d, 3 * d), 0),
            _const_spec((1, d)), _const_spec((1, d)), _const_spec((V7X_MXU_DIM, V7X_MXU_DIM))]


def _qkv_prompt(x, g, w, qg, kg, gmat, tm, nb):
    m, d = x.shape
    batch = m // (tm * nb)
    row = pl.BlockSpec((tm, d), lambda i: (i, 0))
    return pl.pallas_call(
        _qkv_prompt_kernel,
        grid=(m // tm,),
        in_specs=_qkv_in_specs(tm, d),
        out_specs=[row, row, pl.BlockSpec((None, d, tm), lambda i: (i // nb, 0, i % nb)), row,
                   pl.BlockSpec((None, None, d, tm), lambda i: (i // nb, i % nb, 0, 0))],
        out_shape=[jax.ShapeDtypeStruct((m, d), BF16), jax.ShapeDtypeStruct((m, d), BF16),
                   jax.ShapeDtypeStruct((batch, d, tm * nb), F32), jax.ShapeDtypeStruct((m, d), F32),
                   jax.ShapeDtypeStruct((batch, nb, d, tm), BF16)],
        compiler_params=_params("parallel"),
        name="qkv_prompt",
    )(x, g, w, qg, kg, gmat)


def _qkv_sample(x, g, w, qg, kg, gmat):
    m, d = x.shape
    row = pl.BlockSpec((m, d), lambda i: (i, 0))
    return pl.pallas_call(
        _qkv_sample_kernel,
        grid=(1,),
        in_specs=_qkv_in_specs(m, d),
        out_specs=[row, row, row],
        out_shape=[jax.ShapeDtypeStruct((m, d), BF16), jax.ShapeDtypeStruct((m, d), F32),
                   jax.ShapeDtypeStruct((m, d), F32)],
        compiler_params=_params("parallel"),
        name="qkv_sample",
    )(x, g, w, qg, kg, gmat)


def _diff_lambda(lq1, lk1, lq2, lk2, lam_init):
    d1 = jnp.sum(lq1 * lk1, axis=-1, keepdims=True)
    d2 = jnp.sum(lq2 * lk2, axis=-1, keepdims=True)
    return jnp.exp(d1) - jnp.exp(d2) + lam_init


def _decode_scratch(nbuf, pps):
    d = D_MODEL
    nhc = 2 * N_HEADS
    return [pltpu.VMEM((nbuf, pps, d, PAGE_SIZE), F32),
            pltpu.VMEM((nbuf, pps, PAGE_SIZE * N_HEADS, V_DIM), F32),
            pltpu.SemaphoreType.DMA((nbuf,)),
            pltpu.VMEM((d, min(pps, STAGE_PAGES) * PAGE_SIZE), BF16),
            pltpu.VMEM((N_HEADS // 2, min(pps, STAGE_PAGES) * PAGE_SIZE, 2 * V_DIM), BF16),
            pltpu.VMEM((nhc, d), BF16), pltpu.VMEM((nhc, 1), F32),
            pltpu.VMEM((nhc, 1), F32), pltpu.VMEM((N_HEADS // 2, nhc, 2 * V_DIM), F32)]


def _decode_program(pt_ref, q_ref, kn_ref, vn_ref, sub_ref, lq1_ref, lk1_ref, lq2_ref, lk2_ref,
                    kt_hbm, v_hbm, o_ref, kbuf, vbuf, sem, kb_scr, vb_scr, qbd_scr, m_scr, l_scr, acc_scr,
                    *, pps, steps, nbuf, lam_init):
    nhc = 2 * N_HEADS
    total = q_ref.shape[0] * steps

    def chunk_copies(t, slot):
        b = t // steps
        c = t % steps
        copies = []
        for j in range(pps):
            page = pt_ref[b, c * pps + j]
            copies.append(pltpu.make_async_copy(kt_hbm.at[page], kbuf.at[slot, j], sem.at[slot]))
            copies.append(pltpu.make_async_copy(v_hbm.at[page], vbuf.at[slot, j], sem.at[slot]))
        return copies

    def init(b):
        row = lax.broadcasted_iota(jnp.int32, (nhc, D_MODEL), 0)
        col = lax.broadcasted_iota(jnp.int32, (nhc, D_MODEL), 1)
        q = jnp.broadcast_to(q_ref[b].astype(F32), (nhc, D_MODEL))
        qbd = jnp.where(col // HEAD_DIM == row, q, 0.0)
        qbd_scr[...] = qbd.astype(BF16)
        m_scr[...] = jnp.sum(qbd * kn_ref[b], axis=1, keepdims=True)
        l_scr[...] = jnp.ones(l_scr.shape, F32)
        vn = jnp.broadcast_to(vn_ref[b], (nhc, D_MODEL))
        for hp in range(N_HEADS // 2):
            acc_scr[hp] = vn[:, hp * 2 * V_DIM:(hp + 1) * 2 * V_DIM]

    def accumulate(slot):
        groups = [range(g, min(g + STAGE_PAGES, pps)) for g in range(0, pps, STAGE_PAGES)]
        qbd = qbd_scr[...]
        parts = []
        for pages in groups:
            for i, j in enumerate(pages):
                kb_scr[:, i * PAGE_SIZE:(i + 1) * PAGE_SIZE] = kbuf[slot, j].astype(BF16)
            parts.append(_dot(qbd, kb_scr[:, :len(pages) * PAGE_SIZE]))
        s = jnp.concatenate(parts, axis=1)
        m_old = m_scr[...]
        m_new = jnp.maximum(m_old, jnp.max(s, axis=1, keepdims=True))
        alpha = jnp.exp2(m_old - m_new)
        p = jnp.exp2(s - m_new)
        l_scr[...] = alpha * l_scr[...] + jnp.sum(p, axis=1, keepdims=True)
        pb = p.astype(BF16)
        m_scr[...] = m_new
        for hp in range(N_HEADS // 2):
            acc_scr[hp] = alpha * acc_scr[hp]
        for pages in groups:
            n_tok = len(pages) * PAGE_SIZE
            for i, j in enumerate(pages):
                tok = slice(i * PAGE_SIZE, (i + 1) * PAGE_SIZE)
                for h in range(N_HEADS):
                    lanes = slice((h % 2) * V_DIM, (h % 2 + 1) * V_DIM)
                    vb_scr[h // 2, tok, lanes] = vbuf[slot, j, pl.ds(h, PAGE_SIZE, stride=N_HEADS), :].astype(BF16)
            pg = pb[:, pages[0] * PAGE_SIZE:pages[0] * PAGE_SIZE + n_tok]
            for hp in range(N_HEADS // 2):
                acc_scr[hp] += _dot(pg, vb_scr[hp, :n_tok, :])

    def finish(b):
        lam = _diff_lambda(lq1_ref[...], lk1_ref[...], lq2_ref[...], lk2_ref[...], lam_init)
        row = lax.broadcasted_iota(jnp.int32, (nhc, V_DIM), 0)
        inv_l = 1.0 / l_scr[...]
        outs = []
        for h in range(N_HEADS):
            a = acc_scr[h // 2][:, (h % 2) * V_DIM:(h % 2 + 1) * V_DIM] * inv_l
            a = jnp.where(row == 2 * h, a, 0.0) - lam * jnp.where(row == 2 * h + 1, a, 0.0)
            outs.append(jnp.sum(a, axis=0, keepdims=True))
        o = jnp.concatenate(outs, axis=0)
        ms = jnp.mean(o * o, axis=1, keepdims=True)
        o_ref[b] = ((o * lax.rsqrt(ms + EPS)) * sub_ref[...]) * (1.0 - lam_init)

    def prologue():
        for t0 in range(nbuf - 1):
            for cp in chunk_copies(t0, t0):
                cp.start()

    def process(t):
        slot = lax.rem(t, nbuf)
        ahead = t + (nbuf - 1)

        @pl.when(ahead < total)
        def _prefetch():
            for cp in chunk_copies(ahead, lax.rem(ahead, nbuf)):
                cp.start()

        for cp in chunk_copies(t, slot):
            cp.wait()
        b = t // steps
        c = t % steps
        pl.when(c == 0)(lambda: init(b))
        accumulate(slot)
        pl.when(c == steps - 1)(lambda: finish(b))

    return prologue, process


def _decode_kernel(*refs, **cfg):
    prologue, process = _decode_program(*refs, **cfg)
    t = pl.program_id(0)
    pl.when(t == 0)(prologue)
    process(t)


def _attn_sample(q, k_new, v_new, cache_kt, cache_v, page_table, sub, lams, lam_init):
    b = q.shape[0]
    n_pages = page_table.shape[1]
    pps = PAGES_PER_STEP
    nbuf = DECODE_BUFFERS
    assert n_pages % pps == 0
    steps = n_pages // pps

    def whole(shape):
        nd = len(shape)
        return pl.BlockSpec(shape, lambda i, pt: (0,) * nd)

    kern = functools.partial(_decode_kernel, pps=pps, steps=steps, nbuf=nbuf, lam_init=lam_init)
    grid_spec = pltpu.PrefetchScalarGridSpec(
        num_scalar_prefetch=1,
        grid=(b * steps,),
        in_specs=[whole(q.shape), whole(k_new.shape), whole(v_new.shape), whole((1, V_DIM))]
        + [whole((1, HEAD_DIM))] * 4
        + [pl.BlockSpec(memory_space=pl.ANY), pl.BlockSpec(memory_space=pl.ANY)],
        out_specs=whole((b, N_HEADS, V_DIM)),
        scratch_shapes=_decode_scratch(nbuf, pps),
    )
    return pl.pallas_call(
        kern,
        grid_spec=grid_spec,
        out_shape=jax.ShapeDtypeStruct((b, N_HEADS, V_DIM), F32),
        compiler_params=_params("arbitrary"),
        name="attn_sample",
    )(page_table, q, k_new, v_new, sub, *lams, cache_kt, cache_v)


def _split_components(q):
    lane = lax.broadcasted_iota(jnp.int32, q.shape, 1)
    zero = jnp.zeros_like(q)
    return jnp.where(lane < HEAD_DIM, q, zero), jnp.where(lane >= HEAD_DIM, q, zero)


def _causal_keep(shape):
    kpos = lax.broadcasted_iota(jnp.int32, shape, 0)
    qpos = lax.broadcasted_iota(jnp.int32, shape, 1)
    return kpos <= qpos


def _attn_finish(acc0, l0, acc1, l1, sub_ref, lam_refs, o_ref, lam_init):
    lam = _diff_lambda(*(r[...] for r in lam_refs), lam_init)
    o_t = acc0 / l0 - lam * (acc1 / l1)
    ms = jnp.mean(o_t * o_t, axis=0, keepdims=True)
    o_n = (o_t * lax.rsqrt(ms + EPS)).T
    o_ref[...] = ((o_n * sub_ref[...]) * (1.0 - lam_init)).astype(BF16)


def _chunks_per_query_block(nb, chunks):
    weights = [ATTN_FIXED_WORK + i for i in range(nb)]
    ideal = [chunks * w / sum(weights) for w in weights]
    counts = [int(x) for x in ideal]
    by_remainder = sorted(range(nb), key=lambda i: ideal[i] - counts[i], reverse=True)
    for i in by_remainder[:chunks - sum(counts)]:
        counts[i] += 1
    return counts


def _attn_bounded_kernel(pt_ref, q_ref, k_ref, vt_ref, sub_ref, lq1_ref, lk1_ref, lq2_ref, lk2_ref,
                         qs_ref, kn_ref, vn_ref, kt_hbm, v_hbm, o_ref, os_ref,
                         s_even, s_odd, l_scr, acc_scr, *decode_scratch,
                         blk, lam_init, chunks_per_group, decode_cfg):
    half = blk // 2
    qi = pl.program_id(2)
    group = pl.program_id(0) * pl.num_programs(1) + pl.program_id(1)

    prologue, process = _decode_program(
        pt_ref, qs_ref, kn_ref, vn_ref, sub_ref, lq1_ref, lk1_ref, lq2_ref, lk2_ref, kt_hbm, v_hbm, os_ref,
        *decode_scratch, lam_init=lam_init, **decode_cfg)
    counts = _chunks_per_query_block(k_ref.shape[0] // blk, chunks_per_group)
    first = group * chunks_per_group + sum(jnp.where(qi > i, n, 0) for i, n in enumerate(counts))
    count = sum(jnp.where(qi == i, n, 0) for i, n in enumerate(counts))
    pl.when(jnp.logical_and(group == 0, qi == 0))(prologue)

    def decode_body(i, carry):
        process(first + i)
        return carry

    lax.fori_loop(0, count, decode_body, 0)

    qc = _split_components(q_ref[...])
    l_scr[...] = jnp.zeros(l_scr.shape, F32)
    acc_scr[...] = jnp.zeros(acc_scr.shape, F32)

    def scores(j, hf, dst):
        kb = k_ref[pl.ds(pl.multiple_of(j * blk + hf * half, half), half), :]
        for c in range(2):
            dst[c] = _dot_nt(kb, qc[c])

    def consume(j, hf, src, masked):
        vb = vt_ref[j, :, hf * half:(hf + 1) * half]
        for c in range(2):
            p = jnp.exp2(src[c])
            if masked:
                kpos = lax.broadcasted_iota(jnp.int32, p.shape, 0) + hf * half
                qpos = lax.broadcasted_iota(jnp.int32, p.shape, 1)
                p = jnp.where(kpos <= qpos, p, 0.0)
            l_scr[c] += jnp.sum(p.reshape(half // 8, 8, blk), axis=0)
            acc_scr[c] += _dot(vb, p.astype(BF16))

    scores(0, 0, s_even)

    def body(j, carry):
        scores(j, 1, s_odd)
        consume(j, 0, s_even, masked=False)
        scores(j + 1, 0, s_even)
        consume(j, 1, s_odd, masked=False)
        return carry

    lax.fori_loop(0, qi, body, 0)
    kb_hi = k_ref[pl.ds(pl.multiple_of(qi * blk + half, half), half), :]
    for c in range(2):
        s_odd[c, :, half:] = _dot_nt(kb_hi, qc[c][half:, :])
    consume(qi, 0, s_even, masked=True)
    vb_hi = vt_ref[qi, :, half:]
    for c in range(2):
        p = jnp.exp2(s_odd[c, :, half:])
        p = jnp.where(_causal_keep(p.shape), p, 0.0)
        l_scr[c, :, half:] += jnp.sum(p.reshape(half // 8, 8, half), axis=0)
        acc_scr[c, :, half:] += _dot(vb_hi, p.astype(BF16))
    l0 = jnp.sum(l_scr[0], axis=0, keepdims=True)
    l1 = jnp.sum(l_scr[1], axis=0, keepdims=True)
    _attn_finish(acc_scr[0], l0, acc_scr[1], l1, sub_ref, (lq1_ref, lk1_ref, lq2_ref, lk2_ref), o_ref, lam_init)


def _attn_online_kernel(q_ref, k_ref, vt_ref, sub_ref, lq1_ref, lk1_ref, lq2_ref, lk2_ref, o_ref,
                        m_scr, l_scr, acc_scr, *, blk, lam_init):
    qi = pl.program_id(2)
    qc = _split_components(q_ref[...])
    m_scr[...] = jnp.full(m_scr.shape, NEG, F32)
    l_scr[...] = jnp.zeros(l_scr.shape, F32)
    acc_scr[...] = jnp.zeros(acc_scr.shape, F32)

    def update(j, masked):
        kb = k_ref[pl.ds(pl.multiple_of(j * blk, blk), blk), :]
        vb = vt_ref[j]
        for c in range(2):
            s = _dot_nt(kb, qc[c])
            if masked:
                s = jnp.where(_causal_keep(s.shape), s, NEG)
            m_old = m_scr[c]
            m_new = jnp.maximum(m_old, jnp.max(s, axis=0, keepdims=True))
            alpha = jnp.exp2(m_old - m_new)
            p = jnp.exp2(s - m_new)
            l_scr[c] = alpha * l_scr[c] + jnp.sum(p, axis=0, keepdims=True)
            acc_scr[c] = alpha * acc_scr[c] + _dot(vb, p.astype(BF16))
            m_scr[c] = m_new

    def body(j, carry):
        update(j, masked=False)
        return carry

    lax.fori_loop(0, qi, body, 0)
    update(qi, masked=True)
    _attn_finish(acc_scr[0], l_scr[0], acc_scr[1], l_scr[1], sub_ref, (lq1_ref, lk1_ref, lq2_ref, lk2_ref),
                 o_ref, lam_init)


def _attn_prompt_online(q, k, vt, sub, lams, batch, seq, lam_init):
    m, d = q.shape
    blk = ATTN_BLOCK
    nb = seq // blk
    small = _const_spec((1, HEAD_DIM))
    return pl.pallas_call(
        functools.partial(_attn_online_kernel, blk=blk, lam_init=lam_init),
        grid=(batch, N_HEADS, nb),
        in_specs=[pl.BlockSpec((blk, V_DIM), lambda b, h, i: (b * nb + i, h)),
                  pl.BlockSpec((seq, V_DIM), lambda b, h, i: (b, h)),
                  pl.BlockSpec((None, nb, V_DIM, blk), lambda b, h, i: (b, 0, h, 0)),
                  _const_spec((1, V_DIM)), small, small, small, small],
        out_specs=pl.BlockSpec((blk, V_DIM), lambda b, h, i: (b * nb + i, h)),
        out_shape=jax.ShapeDtypeStruct((m, d), BF16),
        scratch_shapes=[pltpu.VMEM((2, 1, blk), F32), pltpu.VMEM((2, 1, blk), F32),
                        pltpu.VMEM((2, V_DIM, blk), F32)],
        compiler_params=_params("parallel", "parallel", "arbitrary"),
        name="attn_prompt_online",
    )(q, k, vt, sub, *lams)


def _attn_prompt_with_decode(q, k, vt, sub, lams, q_s, k_new, v_new, cache_kt, cache_v, page_table,
                             batch, seq, lam_init):
    m, d = q.shape
    blk = ATTN_BLOCK
    nb = seq // blk
    dec_b = q_s.shape[0]
    n_pages = page_table.shape[1]
    pps = PAGES_PER_STEP
    nbuf = FUSED_DECODE_BUFFERS
    groups = batch * N_HEADS
    assert n_pages % pps == 0 and (dec_b * (n_pages // pps)) % groups == 0
    steps = n_pages // pps

    def whole(shape):
        nd = len(shape)
        return pl.BlockSpec(shape, lambda b, h, i, pt: (0,) * nd)

    kern = functools.partial(
        _attn_bounded_kernel, blk=blk, lam_init=lam_init, chunks_per_group=dec_b * steps // groups,
        decode_cfg=dict(pps=pps, steps=steps, nbuf=nbuf))
    stage = pltpu.VMEM((2, blk // 2, blk), F32)
    grid_spec = pltpu.PrefetchScalarGridSpec(
        num_scalar_prefetch=1,
        grid=(batch, N_HEADS, nb),
        in_specs=[pl.BlockSpec((blk, V_DIM), lambda b, h, i, pt: (b * nb + i, h)),
                  pl.BlockSpec((seq, V_DIM), lambda b, h, i, pt: (b, h)),
                  pl.BlockSpec((None, nb, V_DIM, blk), lambda b, h, i, pt: (b, 0, h, 0)),
                  whole((1, V_DIM))] + [whole((1, HEAD_DIM))] * 4
        + [whole(q_s.shape), whole(k_new.shape), whole(v_new.shape),
           pl.BlockSpec(memory_space=pl.ANY), pl.BlockSpec(memory_space=pl.ANY)],
        out_specs=[pl.BlockSpec((blk, V_DIM), lambda b, h, i, pt: (b * nb + i, h)),
                   whole((dec_b, N_HEADS, V_DIM))],
        scratch_shapes=[stage, stage, pltpu.VMEM((2, 8, blk), F32), pltpu.VMEM((2, V_DIM, blk), F32)]
        + _decode_scratch(nbuf, pps),
    )
    return pl.pallas_call(
        kern,
        grid_spec=grid_spec,
        out_shape=[jax.ShapeDtypeStruct((m, d), BF16), jax.ShapeDtypeStruct((dec_b, N_HEADS, V_DIM), F32)],
        compiler_params=_params("arbitrary", "arbitrary", "arbitrary"),
        name="attn_prompt_decode",
    )(page_table, q, k, vt, sub, *lams, q_s, k_new, v_new, cache_kt, cache_v)


def _post_kernel(x_ref, a_ref, wmix_ref, g_ref, wup_ref, wdown_ref, o_ref):
    x1 = x_ref[...] + _dot(a_ref[...], wmix_ref[...])
    h = _rms(x1, g_ref[...]).astype(BF16)
    acc = x1
    ff = wup_ref.shape[1]
    for c in range(ff // FF_CHUNK):
        sl = slice(c * FF_CHUNK, (c + 1) * FF_CHUNK)
        u = jnp.maximum(_dot(h, wup_ref[:, sl]), 0.0)
        acc = acc + _dot((u * u).astype(BF16), wdown_ref[sl, :])
    o_ref[...] = acc


def _post(x, a, w_mix, g, w_up, w_down, layer, tm):
    m, d = x.shape
    row = pl.BlockSpec((tm, d), lambda i: (i, 0))
    return pl.pallas_call(
        _post_kernel,
        grid=(m // tm,),
        in_specs=[row, row, _layer_spec(w_mix.shape, 0), _const_spec((1, d)), _layer_spec(w_up.shape, layer),
                  _layer_spec(w_down.shape, layer)],
        out_specs=row,
        out_shape=jax.ShapeDtypeStruct((m, d), F32),
        compiler_params=_params("parallel"),
        name="post",
    )(x, a, w_mix, g, w_up, w_down)


def _conv_gates(x, g, w_ref):
    d = D_MODEL
    h = _rms(x, g).astype(BF16)
    g_b = _dot(h, w_ref[:, 0:d])
    u = _dot(h, w_ref[:, d:2 * d]) * _dot(h, w_ref[:, 2 * d:3 * d])
    return g_b, u


def _conv_prompt_kernel(x_ref, g_ref, w_ref, cw_ref, a_ref, st_ref, tail_scr, *, seq_blocks):
    i = pl.program_id(0)

    @pl.when(i % seq_blocks == 0)
    def _start_of_sequence():
        tail_scr[...] = jnp.zeros(tail_scr.shape, F32)

    d = D_MODEL
    h = _rms(x_ref[...], g_ref[...]).astype(BF16)
    tm = h.shape[0]
    rows = lax.broadcasted_iota(jnp.int32, (tm, CONV_COLS), 0)
    cw = cw_ref[...]
    for cc in range(d // CONV_COLS):
        lo, hi = cc * CONV_COLS, (cc + 1) * CONV_COLS
        u = _dot(h, w_ref[:, d + lo:d + hi]) * _dot(h, w_ref[:, 2 * d + lo:2 * d + hi])
        t0 = tail_scr[0:1, lo:hi]
        t1 = tail_scr[1:2, lo:hi]
        um1 = jnp.where(rows == 0, t1, pltpu.roll(u, 1, axis=0))
        um2 = jnp.where(rows == 0, t0, jnp.where(rows == 1, t1, pltpu.roll(u, 2, axis=0)))
        conv = cw[0:1, lo:hi] * um2 + cw[1:2, lo:hi] * um1 + cw[2:3, lo:hi] * u
        a_ref[:, lo:hi] = (_dot(h, w_ref[:, lo:hi]) * conv).astype(BF16)
        last = u[tm - 2:tm, :]
        tail_scr[0:2, lo:hi] = last
        st_ref[:, lo:hi] = last


def _conv_prompt(x, g, w_in, conv_w, batch, seq, tm):
    m, d = x.shape
    nb = seq // tm
    row = pl.BlockSpec((tm, d), lambda i: (i, 0))
    kern = functools.partial(_conv_prompt_kernel, seq_blocks=nb)
    return pl.pallas_call(
        kern,
        grid=(m // tm,),
        in_specs=[row, _const_spec((1, d)), _layer_spec(w_in.shape, 0), _const_spec(conv_w.shape)],
        out_specs=[row, pl.BlockSpec((None, 2, d), lambda i: (i // nb, 0, 0))],
        out_shape=[jax.ShapeDtypeStruct((m, d), BF16), jax.ShapeDtypeStruct((batch, 2, d), F32)],
        scratch_shapes=[pltpu.VMEM((8, d), F32)],
        compiler_params=_params("arbitrary"),
        name="conv_prompt",
    )(x, g, w_in, conv_w)


def _conv_sample_kernel(x_ref, s0_ref, s1_ref, g_ref, w_ref, cw_ref, a_ref, u_ref):
    g_b, u = _conv_gates(x_ref[...], g_ref[...], w_ref)
    cw = cw_ref[...]
    conv = cw[0:1, :] * s0_ref[...] + cw[1:2, :] * s1_ref[...] + cw[2:3, :] * u
    a_ref[...] = (g_b * conv).astype(BF16)
    u_ref[...] = u


def _conv_sample(x, s0, s1, g, w_in, conv_w):
    m, d = x.shape
    full = _const_spec((m, d))
    return pl.pallas_call(
        _conv_sample_kernel,
        grid=(1,),
        in_specs=[full, full, full, _const_spec((1, d)), _layer_spec(w_in.shape, 0), _const_spec(conv_w.shape)],
        out_specs=[full, full],
        out_shape=[jax.ShapeDtypeStruct((m, d), BF16), jax.ShapeDtypeStruct((m, d), F32)],
        compiler_params=_params("arbitrary"),
        name="conv_sample",
    )(x, s0, s1, g, w_in, conv_w)


def kernel(x_prompt, x_sample, cache_k, cache_v, state_conv, page_table, norm_mix, w_qkv, q_gain, k_gain,
           lambda_q1, lambda_k1, lambda_q2, lambda_k2, subln, w_o, w_in, conv_w, w_out, norm_mlp, w_up, w_down):
    batch, seq, d = x_prompt.shape
    dec_batch, dec_seq, _ = x_sample.shape
    assert d == D_MODEL and dec_seq == 1 and seq % ATTN_BLOCK == 0 and ROW_BLOCK == ATTN_BLOCK
    n_pool = cache_k.shape[1]
    m_p = batch * seq
    m_s = dec_batch * dec_seq
    tm = ROW_BLOCK

    xp = x_prompt.reshape(m_p, d)
    xs = x_sample.reshape(m_s, d)

    lam_init = _lambda_init(0)
    g0 = norm_mix[0].reshape(1, d)
    wqkv = _to_bf16(w_qkv)
    reps = d // HEAD_DIM
    qg = (jnp.tile(q_gain[0], reps) * (SCALE * LOG2E)).reshape(1, d)
    kg = jnp.tile(k_gain[0], reps).reshape(1, d)
    idx = jnp.arange(V7X_MXU_DIM) // HEAD_DIM
    gmat = (idx[:, None] == idx[None, :]).astype(BF16)
    lams = [a[0].reshape(1, HEAD_DIM) for a in (lambda_q1, lambda_k1, lambda_q2, lambda_k2)]
    sub = subln[0].reshape(1, V_DIM)

    q_p, kb_p, kt_p, vf_p, vt_p = _qkv_prompt(xp, g0, wqkv, qg, kg, gmat, tm, seq // tm)
    score_bound = HEAD_DIM * SCALE * LOG2E * jnp.max(jnp.abs(q_gain[0])) * jnp.max(jnp.abs(k_gain[0]))

    q_s, kf_s, vf_s = _qkv_sample(xs, g0, wqkv, qg, kg, gmat)
    cache_kt = jnp.transpose(cache_k[0], (0, 2, 3, 4, 1)).reshape(n_pool, d, PAGE_SIZE)
    cache_vr = cache_v[0].reshape(n_pool, PAGE_SIZE * N_HEADS, V_DIM)

    def attend_bounded(q, k, vt, sb, lm, qs, kn, vn, ckt, cvr, pt):
        return tuple(_attn_prompt_with_decode(q, k, vt, sb, lm, qs, kn, vn, ckt, cvr, pt, batch, seq, lam_init))

    def attend_any(q, k, vt, sb, lm, qs, kn, vn, ckt, cvr, pt):
        return (_attn_prompt_online(q, k, vt, sb, lm, batch, seq, lam_init),
                _attn_sample(qs, kn, vn, ckt, cvr, pt, sb, lm, lam_init))

    o_p, o_s = lax.cond(
        score_bound <= MAX_UNSHIFTED_LOG2_SCORE, attend_bounded, attend_any,
        q_p, kb_p, vt_p, sub, lams, q_s.reshape(m_s, 1, d), kf_s.reshape(m_s, 1, d), vf_s.reshape(m_s, 1, d),
        cache_kt, cache_vr, page_table)
    o_s = o_s.reshape(m_s, d).astype(BF16)

    wo = _to_bf16(w_o)
    wup = _to_bf16(w_up)
    wdown = _to_bf16(w_down)
    gm0 = norm_mlp[0].reshape(1, d)
    xp = _post(xp, o_p, wo, gm0, wup, wdown, 0, tm)
    xs = _post(xs, o_s, wo, gm0, wup, wdown, 0, m_s)

    g1 = norm_mix[1].reshape(1, d)
    win = _to_bf16(w_in)
    a_p, conv_p = _conv_prompt(xp, g1, win, conv_w[0], batch, seq, tm)
    a_s, u_s = _conv_sample(xs, state_conv[0, :, 0, :], state_conv[0, :, 1, :], g1, win, conv_w[0])

    wout = _to_bf16(w_out)
    gm1 = norm_mlp[1].reshape(1, d)
    xp = _post(xp, a_p, wout, gm1, wup, wdown, 1, tm)
    xs = _post(xs, a_s, wout, gm1, wup, wdown, 1, m_s)

    conv_s = jnp.stack([state_conv[0, :, 1, :], u_s], axis=1)

    return (xp.reshape(batch, seq, d), xs.reshape(dec_batch, dec_seq, d),
            kt_p.reshape(1, batch, N_HEADS, 2, HEAD_DIM, seq).transpose(0, 1, 5, 2, 3, 4),
            vf_p.reshape(1, batch, seq, N_HEADS, V_DIM),
            kf_s.reshape(1, dec_batch, dec_seq, N_HEADS, 2, HEAD_DIM),
            vf_s.reshape(1, dec_batch, dec_seq, N_HEADS, V_DIM),
            conv_p.reshape(1, batch, 2, d), conv_s.reshape(1, dec_batch, 2, d))
```

```python
import functools
import math

import jax
import jax.numpy as jnp
from jax import lax
from jax.experimental import pallas as pl
from jax.experimental.pallas import tpu as pltpu

F32 = jnp.float32
BF16 = jnp.bfloat16

N_HEADS = 8
HEAD_DIM = 64
V_DIM = 2 * HEAD_DIM
D_MODEL = N_HEADS * V_DIM
EPS = 1e-6
SCALE = HEAD_DIM ** -0.5
NEG = -1e30
LOG2E = math.log2(math.e)
MAX_UNSHIFTED_LOG2_SCORE = 64.0
PAGE_SIZE = 128

V7X_MXU_DIM = 256
V7X_VMEM_BYTES = 64 * 1024 * 1024
V7X_VMEM_LIMIT_BYTES = V7X_VMEM_BYTES - 8 * 1024 * 1024

ROW_BLOCK = 512
ATTN_BLOCK = 512
PAGES_PER_STEP = 8
STAGE_PAGES = 8
DECODE_BUFFERS = 3
FUSED_DECODE_BUFFERS = 4
ATTN_FIXED_WORK = 1.7
FF_CHUNK = 1024
CAST_BLOCK_BYTES = 4 * 1024 * 1024
CONV_COLS = 256


def _lambda_init(layer):
    return 0.8 - 0.6 * math.exp(-0.3 * layer)


def _rms(x, g):
    ms = jnp.mean(x * x, axis=-1, keepdims=True)
    return x * lax.rsqrt(ms + EPS) * g


def _dot(a, b):
    return jnp.dot(a, b, preferred_element_type=F32)


def _dot_nt(a, b):
    return lax.dot_general(a, b, (((1,), (1,)), ((), ())), preferred_element_type=F32)


def _const_spec(shape):
    nd = len(shape)
    return pl.BlockSpec(shape, lambda *_: (0,) * nd, pipeline_mode=pl.Buffered(1))


def _layer_spec(shape, layer):
    return pl.BlockSpec((None,) + tuple(shape[1:]), lambda *_: (layer, 0, 0), pipeline_mode=pl.Buffered(1))


def _cast_kernel(x_ref, o_ref):
    o_ref[...] = x_ref[...].astype(BF16)


def _to_bf16(w):
    layers, rows, cols = w.shape
    tr = rows
    while tr * cols * 4 > CAST_BLOCK_BYTES and tr % 16 == 0:
        tr //= 2
    spec = pl.BlockSpec((None, tr, cols), lambda l, i: (l, i, 0))
    return pl.pallas_call(
        _cast_kernel,
        grid=(layers, rows // tr),
        in_specs=[spec],
        out_specs=spec,
        out_shape=jax.ShapeDtypeStruct(w.shape, BF16),
        compiler_params=_params("parallel", "parallel"),
        name="cast_bf16",
    )(w)


def _params(*sem):
    return pltpu.CompilerParams(dimension_semantics=sem, vmem_limit_bytes=V7X_VMEM_LIMIT_BYTES)


def _group_mean_square(t, gmat):
    sq = t * t
    hi = sq.astype(BF16)
    lo = (sq - hi.astype(F32)).astype(BF16)
    parts = []
    for c in range(t.shape[1] // V7X_MXU_DIM):
        sl = slice(c * V7X_MXU_DIM, (c + 1) * V7X_MXU_DIM)
        parts.append(_dot(hi[:, sl], gmat) + _dot(lo[:, sl], gmat))
    return jnp.concatenate(parts, axis=1) * (1.0 / HEAD_DIM)


def _qkv_values(x_ref, g_ref, w_ref, qg_ref, kg_ref, gmat_ref):
    d = D_MODEL
    h = _rms(x_ref[...], g_ref[...]).astype(BF16)
    gmat = gmat_ref[...]
    tq = _dot(h, w_ref[:, 0:d])
    qn = tq * lax.rsqrt(_group_mean_square(tq, gmat) + EPS) * qg_ref[...]
    tk = _dot(h, w_ref[:, d:2 * d])
    kn = tk * lax.rsqrt(_group_mean_square(tk, gmat) + EPS) * kg_ref[...]
    tv = _dot(h, w_ref[:, 2 * d:3 * d])
    return qn, kn, tv


def _qkv_prompt_kernel(x_ref, g_ref, w_ref, qg_ref, kg_ref, gmat_ref, q_ref, kb_ref, kt_ref, vf_ref, vt_ref):
    qn, kn, tv = _qkv_values(x_ref, g_ref, w_ref, qg_ref, kg_ref, gmat_ref)
    q_ref[...] = qn.astype(BF16)
    kb_ref[...] = kn.astype(BF16)
    kt_ref[...] = kn.T
    vf_ref[...] = tv
    vt_ref[...] = tv.T.astype(BF16)


def _qkv_sample_kernel(x_ref, g_ref, w_ref, qg_ref, kg_ref, gmat_ref, q_ref, kf_ref, vf_ref):
    qn, kn, tv = _qkv_values(x_ref, g_ref, w_ref, qg_ref, kg_ref, gmat_ref)
    q_ref[...] = qn.astype(BF16)
    kf_ref[...] = kn
    vf_ref[...] = tv


def _qkv_in_specs(tm, d):
    return [pl.BlockSpec((tm, d), lambda i: (i, 0)), _const_spec((1, d)), _layer_spec((1, d, 3 * d), 0),
            _const_spec((1, d)), _const_spec((1, d)), _const_spec((V7X_MXU_DIM, V7X_MXU_DIM))]


def _qkv_prompt(x, g, w, qg, kg, gmat, tm, nb):
    m, d = x.shape
    batch = m // (tm * nb)
    row = pl.BlockSpec((tm, d), lambda i: (i, 0))
    return pl.pallas_call(
        _qkv_prompt_kernel,
        grid=(m // tm,),
        in_specs=_qkv_in_specs(tm, d),
        out_specs=[row, row, pl.BlockSpec((None, d, tm), lambda i: (i // nb, 0, i % nb)), row,
                   pl.BlockSpec((None, None, d, tm), lambda i: (i // nb, i % nb, 0, 0))],
        out_shape=[jax.ShapeDtypeStruct((m, d), BF16), jax.ShapeDtypeStruct((m, d), BF16),
                   jax.ShapeDtypeStruct((batch, d, tm * nb), F32), jax.ShapeDtypeStruct((m, d), F32),
                   jax.ShapeDtypeStruct((batch, nb, d, tm), BF16)],
        compiler_params=_params("parallel"),
        name="qkv_prompt",
    )(x, g, w, qg, kg, gmat)


def _qkv_sample(x, g, w, qg, kg, gmat):
    m, d = x.shape
    row = pl.BlockSpec((m, d), lambda i: (i, 0))
    return pl.pallas_call(
        _qkv_sample_kernel,
        grid=(1,),
        in_specs=_qkv_in_specs(m, d),
        out_specs=[row, row, row],
        out_shape=[jax.ShapeDtypeStruct((m, d), BF16), jax.ShapeDtypeStruct((m, d), F32),
                   jax.ShapeDtypeStruct((m, d), F32)],
        compiler_params=_params("parallel"),
        name="qkv_sample",
    )(x, g, w, qg, kg, gmat)


def _diff_lambda(lq1, lk1, lq2, lk2, lam_init):
    d1 = jnp.sum(lq1 * lk1, axis=-1, keepdims=True)
    d2 = jnp.sum(lq2 * lk2, axis=-1, keepdims=True)
    return jnp.exp(d1) - jnp.exp(d2) + lam_init


def _decode_scratch(nbuf, pps):
    d = D_MODEL
    nhc = 2 * N_HEADS
    return [pltpu.VMEM((nbuf, pps, d, PAGE_SIZE), F32),
            pltpu.VMEM((nbuf, pps, PAGE_SIZE * N_HEADS, V_DIM), F32),
            pltpu.SemaphoreType.DMA((nbuf,)),
            pltpu.VMEM((d, min(pps, STAGE_PAGES) * PAGE_SIZE), BF16),
            pltpu.VMEM((N_HEADS // 2, min(pps, STAGE_PAGES) * PAGE_SIZE, 2 * V_DIM), BF16),
            pltpu.VMEM((nhc, d), BF16), pltpu.VMEM((nhc, 1), F32),
            pltpu.VMEM((nhc, 1), F32), pltpu.VMEM((N_HEADS // 2, nhc, 2 * V_DIM), F32)]


def _decode_program(pt_ref, q_ref, kn_ref, vn_ref, sub_ref, lq1_ref, lk1_ref, lq2_ref, lk2_ref,
                    kt_hbm, v_hbm, o_ref, kbuf, vbuf, sem, kb_scr, vb_scr, qbd_scr, m_scr, l_scr, acc_scr,
                    *, pps, steps, nbuf, lam_init):
    nhc = 2 * N_HEADS
    total = q_ref.shape[0] * steps

    def chunk_copies(t, slot):
        b = t // steps
        c = t % steps
        copies = []
        for j in range(pps):
            page = pt_ref[b, c * pps + j]
            copies.append(pltpu.make_async_copy(kt_hbm.at[page], kbuf.at[slot, j], sem.at[slot]))
            copies.append(pltpu.make_async_copy(v_hbm.at[page], vbuf.at[slot, j], sem.at[slot]))
        return copies

    def init(b):
        row = lax.broadcasted_iota(jnp.int32, (nhc, D_MODEL), 0)
        col = lax.broadcasted_iota(jnp.int32, (nhc, D_MODEL), 1)
        q = jnp.broadcast_to(q_ref[b].astype(F32), (nhc, D_MODEL))
        qbd = jnp.where(col // HEAD_DIM == row, q, 0.0)
        qbd_scr[...] = qbd.astype(BF16)
        m_scr[...] = jnp.sum(qbd * kn_ref[b], axis=1, keepdims=True)
        l_scr[...] = jnp.ones(l_scr.shape, F32)
        vn = jnp.broadcast_to(vn_ref[b], (nhc, D_MODEL))
        for hp in range(N_HEADS // 2):
            acc_scr[hp] = vn[:, hp * 2 * V_DIM:(hp + 1) * 2 * V_DIM]

    def accumulate(slot):
        groups = [range(g, min(g + STAGE_PAGES, pps)) for g in range(0, pps, STAGE_PAGES)]
        qbd = qbd_scr[...]
        parts = []
        for pages in groups:
            for i, j in enumerate(pages):
                kb_scr[:, i * PAGE_SIZE:(i + 1) * PAGE_SIZE] = kbuf[slot, j].astype(BF16)
            parts.append(_dot(qbd, kb_scr[:, :len(pages) * PAGE_SIZE]))
        s = jnp.concatenate(parts, axis=1)
        m_old = m_scr[...]
        m_new = jnp.maximum(m_old, jnp.max(s, axis=1, keepdims=True))
        alpha = jnp.exp2(m_old - m_new)
        p = jnp.exp2(s - m_new)
        l_scr[...] = alpha * l_scr[...] + jnp.sum(p, axis=1, keepdims=True)
        pb = p.astype(BF16)
        m_scr[...] = m_new
        for hp in range(N_HEADS // 2):
            acc_scr[hp] = alpha * acc_scr[hp]
        for pages in groups:
            n_tok = len(pages) * PAGE_SIZE
            for i, j in enumerate(pages):
                tok = slice(i * PAGE_SIZE, (i + 1) * PAGE_SIZE)
                for h in range(N_HEADS):
                    lanes = slice((h % 2) * V_DIM, (h % 2 + 1) * V_DIM)
                    vb_scr[h // 2, tok, lanes] = vbuf[slot, j, pl.ds(h, PAGE_SIZE, stride=N_HEADS), :].astype(BF16)
            pg = pb[:, pages[0] * PAGE_SIZE:pages[0] * PAGE_SIZE + n_tok]
            for hp in range(N_HEADS // 2):
                acc_scr[hp] += _dot(pg, vb_scr[hp, :n_tok, :])

    def finish(b):
        lam = _diff_lambda(lq1_ref[...], lk1_ref[...], lq2_ref[...], lk2_ref[...], lam_init)
        row = lax.broadcasted_iota(jnp.int32, (nhc, V_DIM), 0)
        inv_l = 1.0 / l_scr[...]
        outs = []
        for h in range(N_HEADS):
            a = acc_scr[h // 2][:, (h % 2) * V_DIM:(h % 2 + 1) * V_DIM] * inv_l
            a = jnp.where(row == 2 * h, a, 0.0) - lam * jnp.where(row == 2 * h + 1, a, 0.0)
            outs.append(jnp.sum(a, axis=0, keepdims=True))
        o = jnp.concatenate(outs, axis=0)
        ms = jnp.mean(o * o, axis=1, keepdims=True)
        o_ref[b] = ((o * lax.rsqrt(ms + EPS)) * sub_ref[...]) * (1.0 - lam_init)

    def prologue():
        for t0 in range(nbuf - 1):
            for cp in chunk_copies(t0, t0):
                cp.start()

    def process(t):
        slot = lax.rem(t, nbuf)
        ahead = t + (nbuf - 1)

        @pl.when(ahead < total)
        def _prefetch():
            for cp in chunk_copies(ahead, lax.rem(ahead, nbuf)):
                cp.start()

        for cp in chunk_copies(t, slot):
            cp.wait()
        b = t // steps
        c = t % steps
        pl.when(c == 0)(lambda: init(b))
        accumulate(slot)
        pl.when(c == steps - 1)(lambda: finish(b))

    return prologue, process


def _decode_kernel(*refs, **cfg):
    prologue, process = _decode_program(*refs, **cfg)
    t = pl.program_id(0)
    pl.when(t == 0)(prologue)
    process(t)


def _attn_sample(q, k_new, v_new, cache_kt, cache_v, page_table, sub, lams, lam_init):
    b = q.shape[0]
    n_pages = page_table.shape[1]
    pps = PAGES_PER_STEP
    nbuf = DECODE_BUFFERS
    assert n_pages % pps == 0
    steps = n_pages // pps

    def whole(shape):
        nd = len(shape)
        return pl.BlockSpec(shape, lambda i, pt: (0,) * nd)

    kern = functools.partial(_decode_kernel, pps=pps, steps=steps, nbuf=nbuf, lam_init=lam_init)
    grid_spec = pltpu.PrefetchScalarGridSpec(
        num_scalar_prefetch=1,
        grid=(b * steps,),
        in_specs=[whole(q.shape), whole(k_new.shape), whole(v_new.shape), whole((1, V_DIM))]
        + [whole((1, HEAD_DIM))] * 4
        + [pl.BlockSpec(memory_space=pl.ANY), pl.BlockSpec(memory_space=pl.ANY)],
        out_specs=whole((b, N_HEADS, V_DIM)),
        scratch_shapes=_decode_scratch(nbuf, pps),
    )
    return pl.pallas_call(
        kern,
        grid_spec=grid_spec,
        out_shape=jax.ShapeDtypeStruct((b, N_HEADS, V_DIM), F32),
        compiler_params=_params("arbitrary"),
        name="attn_sample",
    )(page_table, q, k_new, v_new, sub, *lams, cache_kt, cache_v)


def _split_components(q):
    lane = lax.broadcasted_iota(jnp.int32, q.shape, 1)
    zero = jnp.zeros_like(q)
    return jnp.where(lane < HEAD_DIM, q, zero), jnp.where(lane >= HEAD_DIM, q, zero)


def _causal_keep(shape):
    kpos = lax.broadcasted_iota(jnp.int32, shape, 0)
    qpos = lax.broadcasted_iota(jnp.int32, shape, 1)
    return kpos <= qpos


def _attn_finish(acc0, l0, acc1, l1, sub_ref, lam_refs, o_ref, lam_init):
    lam = _diff_lambda(*(r[...] for r in lam_refs), lam_init)
    o_t = acc0 / l0 - lam * (acc1 / l1)
    ms = jnp.mean(o_t * o_t, axis=0, keepdims=True)
    o_n = (o_t * lax.rsqrt(ms + EPS)).T
    o_ref[...] = ((o_n * sub_ref[...]) * (1.0 - lam_init)).astype(BF16)


def _chunks_per_query_block(nb, key_blocks_per_query_block, chunks):
    weights = [ATTN_FIXED_WORK + key_blocks_per_query_block * i for i in range(nb)]
    ideal = [chunks * w / sum(weights) for w in weights]
    counts = [int(x) for x in ideal]
    by_remainder = sorted(range(nb), key=lambda i: ideal[i] - counts[i], reverse=True)
    for i in by_remainder[:chunks - sum(counts)]:
        counts[i] += 1
    return counts


def _attn_bounded_kernel(pt_ref, q_ref, k_ref, vt_ref, sub_ref, lq1_ref, lk1_ref, lq2_ref, lk2_ref,
                         qs_ref, kn_ref, vn_ref, kt_hbm, v_hbm, o_ref, os_ref,
                         s_even, s_odd, l_scr, acc_scr, *decode_scratch,
                         blk, lam_init, chunks_per_group, decode_cfg):
    half = blk // 2
    qi = pl.program_id(2)
    group = pl.program_id(0) * pl.num_programs(1) + pl.program_id(1)

    prologue, process = _decode_program(
        pt_ref, qs_ref, kn_ref, vn_ref, sub_ref, lq1_ref, lk1_ref, lq2_ref, lk2_ref, kt_hbm, v_hbm, os_ref,
        *decode_scratch, lam_init=lam_init, **decode_cfg)
    qblk = q_ref.shape[0]
    assert qblk == 2 * blk
    counts = _chunks_per_query_block(k_ref.shape[0] // qblk, qblk // blk, chunks_per_group)
    first = group * chunks_per_group + sum(jnp.where(qi > i, n, 0) for i, n in enumerate(counts))
    count = sum(jnp.where(qi == i, n, 0) for i, n in enumerate(counts))
    pl.when(jnp.logical_and(group == 0, qi == 0))(prologue)

    def decode_body(i, carry):
        process(first + i)
        return carry

    lax.fori_loop(0, count, decode_body, 0)

    qc = _split_components(q_ref[...])
    l_scr[...] = jnp.zeros(l_scr.shape, F32)
    acc_scr[...] = jnp.zeros(acc_scr.shape, F32)

    def keys(j, hf):
        return k_ref[pl.ds(pl.multiple_of(j * blk + hf * half, half), half), :]

    def scores(j, hf, dst):
        kb = keys(j, hf)
        for c in range(2):
            dst[c] = _dot_nt(kb, qc[c])

    def consume(j, hf, src, q_lo=0, key_offset=None):
        vb = vt_ref[j, :, hf * half:(hf + 1) * half]
        for c in range(2):
            p = jnp.exp2(src[c, :, q_lo:])
            if key_offset is not None:
                kpos = lax.broadcasted_iota(jnp.int32, p.shape, 0) + key_offset
                qpos = lax.broadcasted_iota(jnp.int32, p.shape, 1)
                p = jnp.where(kpos <= qpos, p, 0.0)
            l_scr[c, :, q_lo:] += jnp.sum(p.reshape(half // 8, 8, p.shape[1]), axis=0)
            acc_scr[c, :, q_lo:] += _dot(vb, p.astype(BF16))

    scores(0, 0, s_even)

    def body(j, carry):
        scores(j, 1, s_odd)
        consume(j, 0, s_even)
        scores(j + 1, 0, s_even)
        consume(j, 1, s_odd)
        return carry

    j0 = 2 * qi
    lax.fori_loop(0, j0, body, 0)
    scores(j0, 1, s_odd)
    consume(j0, 0, s_even, key_offset=0)
    kb = keys(j0 + 1, 0)
    for c in range(2):
        s_even[c, :, blk:] = _dot_nt(kb, qc[c][blk:, :])
    consume(j0, 1, s_odd, key_offset=half)
    kb = keys(j0 + 1, 1)
    for c in range(2):
        s_odd[c, :, blk:] = _dot_nt(kb, qc[c][blk:, :])
    consume(j0 + 1, 0, s_even, q_lo=blk, key_offset=0)
    consume(j0 + 1, 1, s_odd, q_lo=blk, key_offset=half)
    l0 = jnp.sum(l_scr[0], axis=0, keepdims=True)
    l1 = jnp.sum(l_scr[1], axis=0, keepdims=True)
    _attn_finish(acc_scr[0], l0, acc_scr[1], l1, sub_ref, (lq1_ref, lk1_ref, lq2_ref, lk2_ref), o_ref, lam_init)


def _attn_online_kernel(q_ref, k_ref, vt_ref, sub_ref, lq1_ref, lk1_ref, lq2_ref, lk2_ref, o_ref,
                        m_scr, l_scr, acc_scr, *, blk, lam_init):
    qi = pl.program_id(2)
    qc = _split_components(q_ref[...])
    m_scr[...] = jnp.full(m_scr.shape, NEG, F32)
    l_scr[...] = jnp.zeros(l_scr.shape, F32)
    acc_scr[...] = jnp.zeros(acc_scr.shape, F32)

    def update(j, masked):
        kb = k_ref[pl.ds(pl.multiple_of(j * blk, blk), blk), :]
        vb = vt_ref[j]
        for c in range(2):
            s = _dot_nt(kb, qc[c])
            if masked:
                s = jnp.where(_causal_keep(s.shape), s, NEG)
            m_old = m_scr[c]
            m_new = jnp.maximum(m_old, jnp.max(s, axis=0, keepdims=True))
            alpha = jnp.exp2(m_old - m_new)
            p = jnp.exp2(s - m_new)
            l_scr[c] = alpha * l_scr[c] + jnp.sum(p, axis=0, keepdims=True)
            acc_scr[c] = alpha * acc_scr[c] + _dot(vb, p.astype(BF16))
            m_scr[c] = m_new

    def body(j, carry):
        update(j, masked=False)
        return carry

    lax.fori_loop(0, qi, body, 0)
    update(qi, masked=True)
    _attn_finish(acc_scr[0], l_scr[0], acc_scr[1], l_scr[1], sub_ref, (lq1_ref, lk1_ref, lq2_ref, lk2_ref),
                 o_ref, lam_init)


def _attn_prompt_online(q, k, vt, sub, lams, batch, seq, lam_init):
    m, d = q.shape
    blk = ATTN_BLOCK
    nb = seq // blk
    small = _const_spec((1, HEAD_DIM))
    return pl.pallas_call(
        functools.partial(_attn_online_kernel, blk=blk, lam_init=lam_init),
        grid=(batch, N_HEADS, nb),
        in_specs=[pl.BlockSpec((blk, V_DIM), lambda b, h, i: (b * nb + i, h)),
                  pl.BlockSpec((seq, V_DIM), lambda b, h, i: (b, h)),
                  pl.BlockSpec((None, nb, V_DIM, blk), lambda b, h, i: (b, 0, h, 0)),
                  _const_spec((1, V_DIM)), small, small, small, small],
        out_specs=pl.BlockSpec((blk, V_DIM), lambda b, h, i: (b * nb + i, h)),
        out_shape=jax.ShapeDtypeStruct((m, d), BF16),
        scratch_shapes=[pltpu.VMEM((2, 1, blk), F32), pltpu.VMEM((2, 1, blk), F32),
                        pltpu.VMEM((2, V_DIM, blk), F32)],
        compiler_params=_params("parallel", "parallel", "arbitrary"),
        name="attn_prompt_online",
    )(q, k, vt, sub, *lams)


def _attn_prompt_with_decode(q, k, vt, sub, lams, q_s, k_new, v_new, cache_kt, cache_v, page_table,
                             batch, seq, lam_init):
    m, d = q.shape
    blk = ATTN_BLOCK
    qblk = 2 * blk
    nb = seq // blk
    nqb = seq // qblk
    assert seq % qblk == 0
    dec_b = q_s.shape[0]
    n_pages = page_table.shape[1]
    pps = PAGES_PER_STEP
    nbuf = FUSED_DECODE_BUFFERS
    groups = batch * N_HEADS
    assert n_pages % pps == 0 and (dec_b * (n_pages // pps)) % groups == 0
    steps = n_pages // pps

    def whole(shape):
        nd = len(shape)
        return pl.BlockSpec(shape, lambda b, h, i, pt: (0,) * nd)

    kern = functools.partial(
        _attn_bounded_kernel, blk=blk, lam_init=lam_init, chunks_per_group=dec_b * steps // groups,
        decode_cfg=dict(pps=pps, steps=steps, nbuf=nbuf))
    stage = pltpu.VMEM((2, blk // 2, qblk), F32)
    grid_spec = pltpu.PrefetchScalarGridSpec(
        num_scalar_prefetch=1,
        grid=(batch, N_HEADS, nqb),
        in_specs=[pl.BlockSpec((qblk, V_DIM), lambda b, h, i, pt: (b * nqb + i, h)),
                  pl.BlockSpec((seq, V_DIM), lambda b, h, i, pt: (b, h)),
                  pl.BlockSpec((None, nb, V_DIM, blk), lambda b, h, i, pt: (b, 0, h, 0)),
                  whole((1, V_DIM))] + [whole((1, HEAD_DIM))] * 4
        + [whole(q_s.shape), whole(k_new.shape), whole(v_new.shape),
           pl.BlockSpec(memory_space=pl.ANY), pl.BlockSpec(memory_space=pl.ANY)],
        out_specs=[pl.BlockSpec((qblk, V_DIM), lambda b, h, i, pt: (b * nqb + i, h)),
                   whole((dec_b, N_HEADS, V_DIM))],
        scratch_shapes=[stage, stage, pltpu.VMEM((2, 8, qblk), F32), pltpu.VMEM((2, V_DIM, qblk), F32)]
        + _decode_scratch(nbuf, pps),
    )
    return pl.pallas_call(
        kern,
        grid_spec=grid_spec,
        out_shape=[jax.ShapeDtypeStruct((m, d), BF16), jax.ShapeDtypeStruct((dec_b, N_HEADS, V_DIM), F32)],
        compiler_params=_params("arbitrary", "arbitrary", "arbitrary"),
        name="attn_prompt_decode",
    )(page_table, q, k, vt, sub, *lams, q_s, k_new, v_new, cache_kt, cache_v)


def _post_kernel(x_ref, a_ref, wmix_ref, g_ref, wup_ref, wdown_ref, o_ref):
    x1 = x_ref[...] + _dot(a_ref[...], wmix_ref[...])
    h = _rms(x1, g_ref[...]).astype(BF16)
    acc = x1
    ff = wup_ref.shape[1]
    for c in range(ff // FF_CHUNK):
        sl = slice(c * FF_CHUNK, (c + 1) * FF_CHUNK)
        u = jnp.maximum(_dot(h, wup_ref[:, sl]), 0.0)
        acc = acc + _dot((u * u).astype(BF16), wdown_ref[sl, :])
    o_ref[...] = acc


def _post(x, a, w_mix, g, w_up, w_down, layer, tm):
    m, d = x.shape
    row = pl.BlockSpec((tm, d), lambda i: (i, 0))
    return pl.pallas_call(
        _post_kernel,
        grid=(m // tm,),
        in_specs=[row, row, _layer_spec(w_mix.shape, 0), _const_spec((1, d)), _layer_spec(w_up.shape, layer),
                  _layer_spec(w_down.shape, layer)],
        out_specs=row,
        out_shape=jax.ShapeDtypeStruct((m, d), F32),
        compiler_params=_params("parallel"),
        name="post",
    )(x, a, w_mix, g, w_up, w_down)


def _conv_gates(x, g, w_ref):
    d = D_MODEL
    h = _rms(x, g).astype(BF16)
    g_b = _dot(h, w_ref[:, 0:d])
    u = _dot(h, w_ref[:, d:2 * d]) * _dot(h, w_ref[:, 2 * d:3 * d])
    return g_b, u


def _conv_prompt_kernel(x_ref, g_ref, w_ref, cw_ref, a_ref, st_ref, tail_scr, *, seq_blocks):
    i = pl.program_id(0)

    @pl.when(i % seq_blocks == 0)
    def _start_of_sequence():
        tail_scr[...] = jnp.zeros(tail_scr.shape, F32)

    d = D_MODEL
    h = _rms(x_ref[...], g_ref[...]).astype(BF16)
    tm = h.shape[0]
    rows = lax.broadcasted_iota(jnp.int32, (tm, CONV_COLS), 0)
    cw = cw_ref[...]
    for cc in range(d // CONV_COLS):
        lo, hi = cc * CONV_COLS, (cc + 1) * CONV_COLS
        u = _dot(h, w_ref[:, d + lo:d + hi]) * _dot(h, w_ref[:, 2 * d + lo:2 * d + hi])
        t0 = tail_scr[0:1, lo:hi]
        t1 = tail_scr[1:2, lo:hi]
        um1 = jnp.where(rows == 0, t1, pltpu.roll(u, 1, axis=0))
        um2 = jnp.where(rows == 0, t0, jnp.where(rows == 1, t1, pltpu.roll(u, 2, axis=0)))
        conv = cw[0:1, lo:hi] * um2 + cw[1:2, lo:hi] * um1 + cw[2:3, lo:hi] * u
        a_ref[:, lo:hi] = (_dot(h, w_ref[:, lo:hi]) * conv).astype(BF16)
        last = u[tm - 2:tm, :]
        tail_scr[0:2, lo:hi] = last
        st_ref[:, lo:hi] = last


def _conv_prompt(x, g, w_in, conv_w, batch, seq, tm):
    m, d = x.shape
    nb = seq // tm
    row = pl.BlockSpec((tm, d), lambda i: (i, 0))
    kern = functools.partial(_conv_prompt_kernel, seq_blocks=nb)
    return pl.pallas_call(
        kern,
        grid=(m // tm,),
        in_specs=[row, _const_spec((1, d)), _layer_spec(w_in.shape, 0), _const_spec(conv_w.shape)],
        out_specs=[row, pl.BlockSpec((None, 2, d), lambda i: (i // nb, 0, 0))],
        out_shape=[jax.ShapeDtypeStruct((m, d), BF16), jax.ShapeDtypeStruct((batch, 2, d), F32)],
        scratch_shapes=[pltpu.VMEM((8, d), F32)],
        compiler_params=_params("arbitrary"),
        name="conv_prompt",
    )(x, g, w_in, conv_w)


def _conv_sample_kernel(x_ref, s0_ref, s1_ref, g_ref, w_ref, cw_ref, a_ref, u_ref):
    g_b, u = _conv_gates(x_ref[...], g_ref[...], w_ref)
    cw = cw_ref[...]
    conv = cw[0:1, :] * s0_ref[...] + cw[1:2, :] * s1_ref[...] + cw[2:3, :] * u
    a_ref[...] = (g_b * conv).astype(BF16)
    u_ref[...] = u


def _conv_sample(x, s0, s1, g, w_in, conv_w):
    m, d = x.shape
    full = _const_spec((m, d))
    return pl.pallas_call(
        _conv_sample_kernel,
        grid=(1,),
        in_specs=[full, full, full, _const_spec((1, d)), _layer_spec(w_in.shape, 0), _const_spec(conv_w.shape)],
        out_specs=[full, full],
        out_shape=[jax.ShapeDtypeStruct((m, d), BF16), jax.ShapeDtypeStruct((m, d), F32)],
        compiler_params=_params("arbitrary"),
        name="conv_sample",
    )(x, s0, s1, g, w_in, conv_w)


def kernel(x_prompt, x_sample, cache_k, cache_v, state_conv, page_table, norm_mix, w_qkv, q_gain, k_gain,
           lambda_q1, lambda_k1, lambda_q2, lambda_k2, subln, w_o, w_in, conv_w, w_out, norm_mlp, w_up, w_down):
    batch, seq, d = x_prompt.shape
    dec_batch, dec_seq, _ = x_sample.shape
    assert d == D_MODEL and dec_seq == 1 and seq % ATTN_BLOCK == 0 and ROW_BLOCK == ATTN_BLOCK
    n_pool = cache_k.shape[1]
    m_p = batch * seq
    m_s = dec_batch * dec_seq
    tm = ROW_BLOCK

    xp = x_prompt.reshape(m_p, d)
    xs = x_sample.reshape(m_s, d)

    lam_init = _lambda_init(0)
    g0 = norm_mix[0].reshape(1, d)
    wqkv = _to_bf16(w_qkv)
    reps = d // HEAD_DIM
    qg = (jnp.tile(q_gain[0], reps) * (SCALE * LOG2E)).reshape(1, d)
    kg = jnp.tile(k_gain[0], reps).reshape(1, d)
    idx = jnp.arange(V7X_MXU_DIM) // HEAD_DIM
    gmat = (idx[:, None] == idx[None, :]).astype(BF16)
    lams = [a[0].reshape(1, HEAD_DIM) for a in (lambda_q1, lambda_k1, lambda_q2, lambda_k2)]
    sub = subln[0].reshape(1, V_DIM)

    q_p, kb_p, kt_p, vf_p, vt_p = _qkv_prompt(xp, g0, wqkv, qg, kg, gmat, tm, seq // tm)
    score_bound = HEAD_DIM * SCALE * LOG2E * jnp.max(jnp.abs(q_gain[0])) * jnp.max(jnp.abs(k_gain[0]))

    q_s, kf_s, vf_s = _qkv_sample(xs, g0, wqkv, qg, kg, gmat)
    cache_kt = jnp.transpose(cache_k[0], (0, 2, 3, 4, 1)).reshape(n_pool, d, PAGE_SIZE)
    cache_vr = cache_v[0].reshape(n_pool, PAGE_SIZE * N_HEADS, V_DIM)

    def attend_bounded(q, k, vt, sb, lm, qs, kn, vn, ckt, cvr, pt):
        return tuple(_attn_prompt_with_decode(q, k, vt, sb, lm, qs, kn, vn, ckt, cvr, pt, batch, seq, lam_init))

    def attend_any(q, k, vt, sb, lm, qs, kn, vn, ckt, cvr, pt):
        return (_attn_prompt_online(q, k, vt, sb, lm, batch, seq, lam_init),
                _attn_sample(qs, kn, vn, ckt, cvr, pt, sb, lm, lam_init))

    o_p, o_s = lax.cond(
        score_bound <= MAX_UNSHIFTED_LOG2_SCORE, attend_bounded, attend_any,
        q_p, kb_p, vt_p, sub, lams, q_s.reshape(m_s, 1, d), kf_s.reshape(m_s, 1, d), vf_s.reshape(m_s, 1, d),
        cache_kt, cache_vr, page_table)
    o_s = o_s.reshape(m_s, d).astype(BF16)

    wo = _to_bf16(w_o)
    wup = _to_bf16(w_up)
    wdown = _to_bf16(w_down)
    gm0 = norm_mlp[0].reshape(1, d)
    xp = _post(xp, o_p, wo, gm0, wup, wdown, 0, tm)
    xs = _post(xs, o_s, wo, gm0, wup, wdown, 0, m_s)

    g1 = norm_mix[1].reshape(1, d)
    win = _to_bf16(w_in)
    a_p, conv_p = _conv_prompt(xp, g1, win, conv_w[0], batch, seq, tm)
    a_s, u_s = _conv_sample(xs, state_conv[0, :, 0, :], state_conv[0, :, 1, :], g1, win, conv_w[0])

    wout = _to_bf16(w_out)
    gm1 = norm_mlp[1].reshape(1, d)
    xp = _post(xp, a_p, wout, gm1, wup, wdown, 1, tm)
    xs = _post(xs, a_s, wout, gm1, wup, wdown, 1, m_s)

    conv_s = jnp.stack([state_conv[0, :, 1, :], u_s], axis=1)

    return (xp.reshape(batch, seq, d), xs.reshape(dec_batch, dec_seq, d),
            kt_p.reshape(1, batch, N_HEADS, 2, HEAD_DIM, seq).transpose(0, 1, 5, 2, 3, 4),
            vf_p.reshape(1, batch, seq, N_HEADS, V_DIM),
            kf_s.reshape(1, dec_batch, dec_seq, N_HEADS, 2, HEAD_DIM),
            vf_s.reshape(1, dec_batch, dec_seq, N_HEADS, V_DIM),
            conv_p.reshape(1, batch, 2, d), conv_s.reshape(1, dec_batch, 2, d))
```
